```python
import math
import jax, jax.numpy as jnp
from jax import lax
import numpy as np

D_MODEL = 1024
BATCH = 4
SEQ = 4096
DEPTH = 1
DEC_BATCH = 2
DEC_SEQ = 16384
PAST_LEN = 128

MLA_HEADS = 8
QK_NOPE = 128
QK_ROPE = 64
V_HEAD = 128
Q_LORA = 384
KV_LORA = 256
ROPE_BASE = 10000.0
Q_BLOCK = 128
HY_WIDTH = 1024
HY_ORDER = 2
HY_DIRS = 2
FILT_BANDS = 16
FILT_EMB = 1 + 2 * FILT_BANDS
FILT_HID = 64
FILT_OUT_SCALE = 0.005
DECAY_FAST = 0.3
DECAY_SLOW = 1.5
DECAY_TARGET = 1e-2
DECAY_SHIFT = 0.05
MAX_DECAY = math.log(DECAY_TARGET) / DECAY_FAST
MIN_DECAY = math.log(DECAY_TARGET) / DECAY_SLOW
D_FF = 2816
IN_Q = Q_LORA
IN_KV = KV_LORA
IN_KR = QK_ROPE
IN_HY = 3 * HY_WIDTH
IN_GATE = 2 * D_MODEL
IN_COLS = IN_Q + IN_KV + IN_KR + IN_HY + IN_GATE
IN_SPLITS = (IN_Q, IN_Q + IN_KV, IN_Q + IN_KV + IN_KR, IN_Q + IN_KV + IN_KR + IN_HY)
DN_ALPHA = (2.0 * DEPTH) ** 0.25
DN_BETA = (8.0 * DEPTH) ** -0.25
LN_EPS = 1e-5
RMS_EPS = 1e-6

kernel_name = "hybrid_hyena_mla_deepnorm_encoder"


def layer_norm(x, g, b):
    xf = x.astype(jnp.float32)
    mu = jnp.mean(xf, axis=-1, keepdims=True)
    xc = xf - mu
    var = jnp.mean(xc * xc, axis=-1, keepdims=True)
    return (xc * lax.rsqrt(var + LN_EPS) * g.astype(jnp.float32) + b.astype(jnp.float32)).astype(x.dtype)


def rms_norm(x, g):
    xf = x.astype(jnp.float32)
    ms = jnp.mean(xf * xf, axis=-1, keepdims=True)
    return (xf * lax.rsqrt(ms + RMS_EPS) * g.astype(jnp.float32)).astype(x.dtype)


def dwconv3(u, w, b):
    up = jnp.pad(u, ((0, 0), (1, 1), (0, 0)))
    return up[:, :-2] * w[0] + up[:, 1:-1] * w[1] + up[:, 2:] * w[2] + b


def rope_tables(L):
    pos = jnp.arange(L, dtype=jnp.float32)
    inv = ROPE_BASE ** (-jnp.arange(0, QK_ROPE, 2, dtype=jnp.float32) / QK_ROPE)
    ang = pos[:, None] * inv[None, :]
    return jnp.cos(ang), jnp.sin(ang)


def apply_rope(x, cos, sin):
    x1, x2 = jnp.split(x, 2, axis=-1)
    return jnp.concatenate([x1 * cos - x2 * sin, x1 * sin + x2 * cos], axis=-1)


def mla_branch(c_q, c_kv, k_r, q_norm_g, w_uq, kv_norm_g, w_ukv, w_o_mla):
    B, L, _ = c_q.shape
    q = (rms_norm(c_q, q_norm_g) @ w_uq).reshape(B, L, MLA_HEADS, QK_NOPE + QK_ROPE)
    kv = (rms_norm(c_kv, kv_norm_g) @ w_ukv).reshape(B, L, MLA_HEADS, QK_NOPE + V_HEAD)
    q_nope, q_rope = q[..., :QK_NOPE], q[..., QK_NOPE:]
    k_nope, v = kv[..., :QK_NOPE], kv[..., QK_NOPE:]
    cos, sin = rope_tables(L)
    q_rope = apply_rope(q_rope, cos[None, :, None, :], sin[None, :, None, :])
    k_rope = apply_rope(k_r, cos[None], sin[None])
    scale = (QK_NOPE + QK_ROPE) ** -0.5
    nblk = L // Q_BLOCK
    qn_b = q_nope.reshape(B, nblk, Q_BLOCK, MLA_HEADS, QK_NOPE).transpose(1, 0, 2, 3, 4)
    qr_b = q_rope.reshape(B, nblk, Q_BLOCK, MLA_HEADS, QK_ROPE).transpose(1, 0, 2, 3, 4)

    def attend(blk):
        qn, qr = blk
        s = (jnp.einsum('bqhd,bkhd->bhqk', qn, k_nope).astype(jnp.float32)
             + jnp.einsum('bqhr,bkr->bhqk', qr, k_rope).astype(jnp.float32)) * scale
        p = jax.nn.softmax(s, axis=-1).astype(v.dtype)
        return jnp.einsum('bhqk,bkhd->bqhd', p, v)

    o = lax.map(attend, (qn_b, qr_b))
    o = o.transpose(1, 0, 2, 3, 4).reshape(B, L, MLA_HEADS * V_HEAD)
    return o @ w_o_mla


def hyena_filters(L, filt_w1, filt_b1, filt_freq, filt_w2, filt_b2, filt_w3):
    f32 = jnp.float32
    pos = jnp.arange(L, dtype=f32)
    t = pos / max(L - 1, 1)
    bands = jnp.linspace(1e-4, FILT_BANDS - 1, FILT_BANDS, dtype=f32)
    ang = (2.0 * math.pi * pos / L)[:, None] * bands[None, :]
    z = jnp.concatenate([t[:, None], jnp.cos(ang), -jnp.sin(ang)], axis=-1)
    freq = filt_freq.astype(f32)
    h = jnp.sin(freq * (z @ filt_w1.astype(f32) + filt_b1.astype(f32)))
    h = jnp.sin(freq * (h @ filt_w2.astype(f32) + filt_b2.astype(f32)))
    h = h @ filt_w3.astype(f32)
    deltas = jnp.abs(jnp.linspace(MIN_DECAY, MAX_DECAY, HY_WIDTH, dtype=f32))
    window = jnp.exp(-t[:, None] * deltas[None, :]) + DECAY_SHIFT
    return h.reshape(L, HY_DIRS, HY_ORDER, HY_WIDTH) * window[:, None, None, :]


def long_conv(v, h_fwd, h_bwd, skip):
    L, C = h_fwd.shape
    k = jnp.concatenate([h_fwd, jnp.zeros((1, C), jnp.float32), h_bwd[:0:-1]], axis=0)
    vf = jnp.fft.rfft(v, n=2 * L, axis=1)
    kf = jnp.fft.rfft(k, axis=0)
    y = jnp.fft.irfft(vf * kf[None], n=2 * L, axis=1)[:, :L]
    return y + v * skip.astype(jnp.float32)


def hyena_branch(u, short_w, short_b, filt_w1, filt_b1, filt_freq, filt_w2, filt_b2, filt_w3, hy_skip, w_o_hy):
    L = u.shape[1]
    u = dwconv3(u, short_w, short_b)
    x1, x2, v = jnp.split(u, 3, axis=-1)
    h = hyena_filters(L, filt_w1, filt_b1, filt_freq, filt_w2, filt_b2, filt_w3)
    z = v.astype(jnp.float32)
    for n, gate in enumerate((x1, x2)):
        z = gate.astype(jnp.float32) * long_conv(z, h[:, 0, n], h[:, 1, n], hy_skip[n])
    return z.astype(u.dtype) @ w_o_hy


def encoder_layer(x, w_in, short_w, short_b, q_norm_g, w_uq, kv_norm_g, w_ukv, w_o_mla,
                  filt_w1, filt_b1, filt_freq, filt_w2, filt_b2, filt_w3, hy_skip, w_o_hy,
                  w_out, ln1_g, ln1_b, w_ffn_up, dw_w, dw_b, w_ffn_down, ln2_g, ln2_b):
    proj = x @ w_in
    c_q, c_kv, k_r, u_hy, g = jnp.split(proj, IN_SPLITS, axis=-1)
    o_mla = mla_branch(c_q, c_kv, k_r, q_norm_g, w_uq, kv_norm_g, w_ukv, w_o_mla)
    o_hy = hyena_branch(u_hy, short_w, short_b, filt_w1, filt_b1, filt_freq, filt_w2, filt_b2,
                        filt_w3, hy_skip, w_o_hy)
    g_hy, g_mla = jnp.split(g, 2, axis=-1)
    merged = jax.nn.sigmoid(g_hy) * o_hy + jax.nn.sigmoid(g_mla) * o_mla
    x = layer_norm(DN_ALPHA * x + merged @ w_out, ln1_g, ln1_b)
    a, b = jnp.split(x @ w_ffn_up, 2, axis=-1)
    hmid = jax.nn.gelu(dwconv3(a, dw_w, dw_b), approximate=False) * b
    return layer_norm(DN_ALPHA * x + hmid @ w_ffn_down, ln2_g, ln2_b)


def setup_inputs(seed: int = 0) -> dict:
    key = jax.random.key(seed)
    ks = jax.random.split(key, 32)

    def nrm(k, shape, scale):
        return jax.random.normal(k, shape, jnp.float32) * scale

    def gain(k, shape):
        return 1.0 + nrm(k, shape, 0.02)

    Dp = DEPTH
    return {
        "x_prompt": nrm(ks[0], (BATCH, SEQ, D_MODEL), 1.0),
        "x_sample": nrm(ks[1], (DEC_BATCH, DEC_SEQ, D_MODEL), 1.0),
        "w_in": nrm(ks[2], (Dp, D_MODEL, IN_COLS), D_MODEL ** -0.5),
        "short_w": nrm(ks[3], (Dp, 3, IN_HY), 3 ** -0.5),
        "short_b": nrm(ks[4], (Dp, IN_HY), 0.02),
        "q_norm_g": gain(ks[5], (Dp, Q_LORA)),
        "w_uq": nrm(ks[6], (Dp, Q_LORA, MLA_HEADS * (QK_NOPE + QK_ROPE)), Q_LORA ** -0.5),
        "kv_norm_g": gain(ks[7], (Dp, KV_LORA)),
        "w_ukv": nrm(ks[8], (Dp, KV_LORA, MLA_HEADS * (QK_NOPE + V_HEAD)), KV_LORA ** -0.5),
        "w_o_mla": nrm(ks[9], (Dp, MLA_HEADS * V_HEAD, D_MODEL), (MLA_HEADS * V_HEAD) ** -0.5),
        "filt_w1": nrm(ks[10], (Dp, FILT_EMB, FILT_HID), FILT_EMB ** -0.5),
        "filt_b1": nrm(ks[11], (Dp, FILT_HID), 0.1),
        "filt_freq": gain(ks[12], (Dp, FILT_HID)),
        "filt_w2": nrm(ks[13], (Dp, FILT_HID, FILT_HID), FILT_HID ** -0.5),
        "filt_b2": nrm(ks[14], (Dp, FILT_HID), 0.1),
        "filt_w3": nrm(ks[15], (Dp, FILT_HID, HY_DIRS * HY_ORDER * HY_WIDTH), FILT_OUT_SCALE),
        "hy_skip": nrm(ks[16], (Dp, HY_ORDER, HY_WIDTH), 0.5),
        "w_o_hy": nrm(ks[17], (Dp, HY_WIDTH, D_MODEL), HY_WIDTH ** -0.5),
        "w_out": nrm(ks[18], (Dp, D_MODEL, D_MODEL), D_MODEL ** -0.5 * DN_BETA),
        "ln1_g": gain(ks[19], (Dp, D_MODEL)),
        "ln1_b": nrm(ks[20], (Dp, D_MODEL), 0.02),
        "w_ffn_up": nrm(ks[21], (Dp, D_MODEL, 2 * D_FF), D_MODEL ** -0.5),
        "dw_w": nrm(ks[22], (Dp, 3, D_FF), 3 ** -0.5),
        "dw_b": nrm(ks[23], (Dp, D_FF), 0.02),
        "w_ffn_down": nrm(ks[24], (Dp, D_FF, D_MODEL), D_FF ** -0.5 * DN_BETA),
        "ln2_g": gain(ks[25], (Dp, D_MODEL)),
        "ln2_b": nrm(ks[26], (Dp, D_MODEL), 0.02),
    }


def reference(x_prompt, x_sample, w_in, short_w, short_b, q_norm_g, w_uq, kv_norm_g, w_ukv, w_o_mla,
              filt_w1, filt_b1, filt_freq, filt_w2, filt_b2, filt_w3, hy_skip, w_o_hy,
              w_out, ln1_g, ln1_b, w_ffn_up, dw_w, dw_b, w_ffn_down, ln2_g, ln2_b):
    y_prompt = x_prompt
    y_sample = x_sample
    for i in range(DEPTH):
        lp = (w_in[i], short_w[i], short_b[i], q_norm_g[i], w_uq[i], kv_norm_g[i], w_ukv[i], w_o_mla[i],
              filt_w1[i], filt_b1[i], filt_freq[i], filt_w2[i], filt_b2[i], filt_w3[i], hy_skip[i], w_o_hy[i],
              w_out[i], ln1_g[i], ln1_b[i], w_ffn_up[i], dw_w[i], dw_b[i], w_ffn_down[i], ln2_g[i], ln2_b[i])
        y_prompt = encoder_layer(y_prompt, *lp)
        y_sample = encoder_layer(y_sample, *lp)
    return (y_prompt, y_sample)
```

```python
import functools
import math

import jax
import jax.numpy as jnp
from jax import lax
from jax.experimental import pallas as pl
from jax.experimental.pallas import tpu as pltpu

F32 = jnp.float32
BF16 = jnp.bfloat16

D_MODEL = 1024
DEPTH = 1
MLA_HEADS = 8
QK_NOPE = 128
QK_ROPE = 64
V_HEAD = 128
Q_LORA = 384
KV_LORA = 256
ROPE_BASE = 10000.0
HY_WIDTH = 1024
FILT_BANDS = 16
DECAY_FAST = 0.3
DECAY_SLOW = 1.5
DECAY_TARGET = 1e-2
DECAY_SHIFT = 0.05
MAX_DECAY = math.log(DECAY_TARGET) / DECAY_FAST
MIN_DECAY = math.log(DECAY_TARGET) / DECAY_SLOW
D_FF = 2816
IN_HY = 3 * HY_WIDTH
DN_ALPHA = (2.0 * DEPTH) ** 0.25
LN_EPS = 1e-5
RMS_EPS = 1e-6

HALO = 8
Q_HEAD_COLS = 256
VMEM_LIMIT = 56 * 1024 * 1024


def _cparams(sem):
    return pltpu.CompilerParams(dimension_semantics=sem, vmem_limit_bytes=VMEM_LIMIT)


def _const_spec(shape):
    nd = len(shape)
    return pl.BlockSpec(shape, lambda *_: (0,) * nd, pipeline_mode=pl.Buffered(1))


def _dot(a, b):
    return jnp.dot(a, b, preferred_element_type=F32)


def _layer_norm(y, g, b):
    mu = jnp.mean(y, axis=-1, keepdims=True)
    yc = y - mu
    var = jnp.mean(yc * yc, axis=-1, keepdims=True)
    return yc * lax.rsqrt(var + LN_EPS) * g + b


def _rms_norm(c, g):
    return c * lax.rsqrt(jnp.mean(c * c, axis=-1, keepdims=True) + RMS_EPS) * g


def _rope128(y, cs1, cs2):
    return y * cs1 + pltpu.roll(y, 64, 1) * cs2


def _halo_rows(xp_ref, xm, xn_ref):
    i = pl.program_id(1)
    mp = (i > 0).astype(F32)
    mn = (i < pl.num_programs(1) - 1).astype(F32)
    return jnp.concatenate([xp_ref[...] * mp, xm, xn_ref[...] * mn], axis=0)


def _dwconv3_from(ue_ref, w_ref, b_ref, cols, tm):
    return (ue_ref[HALO - 1:HALO - 1 + tm, :] * w_ref[0:1, cols]
            + ue_ref[HALO:HALO + tm, :] * w_ref[1:2, cols]
            + ue_ref[HALO + 1:HALO + 1 + tm, :] * w_ref[2:3, cols]
            + b_ref[:, cols])


def _front_kernel(xp_ref, xm_ref, xn_ref, cs1_ref, cs2_ref,
                  wq_ref, wkv_ref, wkr_ref, why_ref, wg_ref,
                  qg_ref, wuq_ref, kvg_ref, wukv_ref, sw_ref, sb_ref,
                  q_ref, kv_ref, kr_ref, hy_ref, sg_ref, ue_ref, *, tm, hy_chunk):
    xm = xm_ref[...]
    xmb = xm.astype(BF16)
    cs1 = cs1_ref[...]
    cs2 = cs2_ref[...]
    scale = (QK_NOPE + QK_ROPE) ** -0.5

    cqn = _rms_norm(_dot(xmb, wq_ref[...]), qg_ref[...]).astype(BF16)
    q = _dot(cqn, wuq_ref[...])
    for h in range(MLA_HEADS):
        c0 = h * Q_HEAD_COLS
        q_ref[:, c0:c0 + QK_NOPE] = (q[:, c0:c0 + QK_NOPE] * scale).astype(BF16)
        rr = _rope128(q[:, c0 + QK_NOPE:c0 + Q_HEAD_COLS], cs1, cs2)
        q_ref[:, c0 + QK_NOPE:c0 + Q_HEAD_COLS] = (rr * scale).astype(BF16)

    ckvn = _rms_norm(_dot(xmb, wkv_ref[...]), kvg_ref[...]).astype(BF16)
    kv_ref[...] = _dot(ckvn, wukv_ref[...]).astype(BF16)

    kr_ref[...] = _rope128(_dot(xmb, wkr_ref[...]), cs1, cs2).astype(BF16)

    sg_ref[...] = jax.nn.sigmoid(_dot(xmb, wg_ref[...]))

    xe = _halo_rows(xp_ref, xm, xn_ref).astype(BF16)
    for c in range(0, IN_HY, hy_chunk):
        cols = slice(c, c + hy_chunk)
        ue_ref[...] = _dot(xe, why_ref[:, cols])
        hy_ref[:, cols] = _dwconv3_from(ue_ref, sw_ref, sb_ref, cols, tm)


def _front(x, cs1, cs2, wq, wkv, wkr, why, wg, qg, wuq, kvg, wukv, sw, sb, *, tm):
    B, L, D = x.shape
    nt = L // tm
    hb = tm // HALO
    hy_chunk = 512
    row = lambda w: pl.BlockSpec((None, tm, w), lambda b, i: (b, i, 0))
    in_specs = [
        pl.BlockSpec((None, HALO, D), lambda b, i: (b, jnp.maximum(i * hb - 1, 0), 0)),
        row(D),
        pl.BlockSpec((None, HALO, D), lambda b, i: (b, jnp.minimum((i + 1) * hb, L // HALO - 1), 0)),
        pl.BlockSpec((tm, 128), lambda b, i: (i, 0)),
        pl.BlockSpec((tm, 128), lambda b, i: (i, 0)),
    ] + [_const_spec(w.shape) for w in (wq, wkv, wkr, why, wg, qg, wuq, kvg, wukv, sw, sb)]
    out_shape = (
        jax.ShapeDtypeStruct((B, L, MLA_HEADS * Q_HEAD_COLS), BF16),
        jax.ShapeDtypeStruct((B, L, MLA_HEADS * (QK_NOPE + V_HEAD)), BF16),
        jax.ShapeDtypeStruct((B, L, 128), BF16),
        jax.ShapeDtypeStruct((B, L, IN_HY), F32),
        jax.ShapeDtypeStruct((B, L, 2 * D_MODEL), F32),
    )
    out_specs = tuple(row(s.shape[-1]) for s in out_shape)
    return pl.pallas_call(
        functools.partial(_front_kernel, tm=tm, hy_chunk=hy_chunk),
        grid=(B, nt),
        in_specs=in_specs,
        out_specs=out_specs,
        out_shape=out_shape,
        scratch_shapes=[pltpu.VMEM((tm + 2 * HALO, hy_chunk), F32)],
        compiler_params=_cparams(("parallel", "parallel")),
        name="front",
    )(x, x, x, cs1, cs2, wq, wkv, wkr, why, wg, qg, wuq, kvg, wukv, sw, sb)


def _attn_kernel(q_ref, kn_ref, kr_ref, v_ref, o_ref, m_ref, l_ref, acc_ref, *, tk, nk):
    q = q_ref[...]
    m_ref[...] = jnp.full(m_ref.shape, -jnp.inf, F32)
    l_ref[...] = jnp.zeros(l_ref.shape, F32)
    acc_ref[...] = jnp.zeros(acc_ref.shape, F32)

    def body(j, carry):
        s0 = pl.multiple_of(j * tk, tk)
        kcat = jnp.concatenate([kn_ref[pl.ds(s0, tk), :], kr_ref[pl.ds(s0, tk), :]], axis=-1)
        s = lax.dot_general(q, kcat, (((1,), (1,)), ((), ())), preferred_element_type=F32)
        m_prev = m_ref[...]
        m_new = jnp.maximum(m_prev, jnp.max(s, axis=-1, keepdims=True))
        alpha = jnp.exp(m_prev - m_new)
        p = jnp.exp(s - pltpu.repeat(m_new, tk // 128, axis=1))
        l_ref[...] = alpha * l_ref[...] + jnp.sum(p, axis=-1, keepdims=True)
        acc_ref[...] = alpha * acc_ref[...] + _dot(p.astype(BF16), v_ref[pl.ds(s0, tk), :])
        m_ref[...] = m_new
        return carry

    lax.fori_loop(0, nk, body, 0)
    o_ref[...] = (acc_ref[...] / l_ref[...]).astype(o_ref.dtype)


def _attention(q, kv, kr, *, tq, tk):
    B, L, _ = q.shape
    return pl.pallas_call(
        functools.partial(_attn_kernel, tk=tk, nk=L // tk),
        grid=(B, MLA_HEADS, L // tq),
        in_specs=[
            pl.BlockSpec((None, tq, Q_HEAD_COLS), lambda b, h, i: (b, i, h)),
            pl.BlockSpec((None, L, QK_NOPE), lambda b, h, i: (b, 0, 2 * h)),
            pl.BlockSpec((None, L, 128), lambda b, h, i: (b, 0, 0)),
            pl.BlockSpec((None, L, V_HEAD), lambda b, h, i: (b, 0, 2 * h + 1)),
        ],
        out_specs=pl.BlockSpec((None, tq, V_HEAD), lambda b, h, i: (b, i, h)),
        out_shape=jax.ShapeDtypeStruct((B, L, MLA_HEADS * V_HEAD), BF16),
        scratch_shapes=[pltpu.VMEM((tq, 128), F32), pltpu.VMEM((tq, 128), F32),
                        pltpu.VMEM((tq, V_HEAD), F32)],
        compiler_params=_cparams(("parallel", "parallel", "arbitrary")),
        name="attention",
    )(q, kv, kr, kv)


def _mid_kernel(o_ref, z_ref, sg_ref, x_ref, womla_ref, wohy_ref, wout_ref, g_ref, b_ref, out_ref):
    o_mla = _dot(o_ref[...], womla_ref[...])
    o_hy = _dot(z_ref[...].astype(BF16), wohy_ref[...])
    merged = sg_ref[:, :D_MODEL] * o_hy + sg_ref[:, D_MODEL:] * o_mla
    y = DN_ALPHA * x_ref[...] + _dot(merged.astype(BF16), wout_ref[...])
    out_ref[...] = _layer_norm(y, g_ref[...], b_ref[...])


def _mid(o, z, sg, x, womla, wohy, wout, g, b, *, tm):
    T, D = x.shape
    row = lambda w: pl.BlockSpec((tm, w), lambda i: (i, 0))
    return pl.pallas_call(
        _mid_kernel,
        grid=(T // tm,),
        in_specs=[row(D), row(D), row(2 * D), row(D)]
        + [_const_spec(w.shape) for w in (womla, wohy, wout, g, b)],
        out_specs=row(D),
        out_shape=jax.ShapeDtypeStruct((T, D), F32),
        compiler_params=_cparams(("parallel",)),
        name="mid",
    )(o, z, sg, x, womla, wohy, wout, g, b)


def _ffn_kernel(xp_ref, xm_ref, xn_ref, wa_ref, wb_ref, dww_ref, dwb_ref, wd_ref, g_ref, b_ref,
                out_ref, ae_ref, acc_ref, *, tm, ff_chunk):
    xm = xm_ref[...]
    xmb = xm.astype(BF16)
    xe = _halo_rows(xp_ref, xm, xn_ref).astype(BF16)
    for c in range(0, D_FF, ff_chunk):
        cols = slice(c, c + ff_chunk)
        ae_ref[...] = _dot(xe, wa_ref[:, cols])
        a = _dwconv3_from(ae_ref, dww_ref, dwb_ref, cols, tm)
        gelu = 0.5 * a * (1.0 + lax.erf(a * math.sqrt(0.5)))
        hmid = gelu * _dot(xmb, wb_ref[:, cols])
        part = _dot(hmid.astype(BF16), wd_ref[cols, :])
        if c == 0:
            acc_ref[...] = part
        else:
            acc_ref[...] += part
    out_ref[...] = _layer_norm(DN_ALPHA * xm + acc_ref[...], g_ref[...], b_ref[...])


def _ffn(x, wa, wb, dww, dwb, wd, g, b, *, tm):
    B, L, D = x.shape
    hb = tm // HALO
    ff_chunk = 256
    row = pl.BlockSpec((None, tm, D), lambda b_, i: (b_, i, 0))
    return pl.pallas_call(
        functools.partial(_ffn_kernel, tm=tm, ff_chunk=ff_chunk),
        grid=(B, L // tm),
        in_specs=[
            pl.BlockSpec((None, HALO, D), lambda b_, i: (b_, jnp.maximum(i * hb - 1, 0), 0)),
            row,
            pl.BlockSpec((None, HALO, D), lambda b_, i: (b_, jnp.minimum((i + 1) * hb, L // HALO - 1), 0)),
        ] + [_const_spec(w.shape) for w in (wa, wb, dww, dwb, wd, g, b)],
        out_specs=row,
        out_shape=jax.ShapeDtypeStruct((B, L, D), F32),
        scratch_shapes=[pltpu.VMEM((tm + 2 * HALO, ff_chunk), F32), pltpu.VMEM((tm, D), F32)],
        compiler_params=_cparams(("parallel", "parallel")),
        name="ffn",
    )(x, x, x, wa, wb, dww, dwb, wd, g, b)


def _hyena_filters(L, filt_w1, filt_b1, filt_freq, filt_w2, filt_b2, filt_w3):
    pos = jnp.arange(L, dtype=F32)
    t = pos / max(L - 1, 1)
    bands = jnp.linspace(1e-4, FILT_BANDS - 1, FILT_BANDS, dtype=F32)
    ang = (2.0 * math.pi * pos / L)[:, None] * bands[None, :]
    z = jnp.concatenate([t[:, None], jnp.cos(ang), -jnp.sin(ang)], axis=-1)
    h = jnp.sin(filt_freq * (z @ filt_w1 + filt_b1))
    h = jnp.sin(filt_freq * (h @ filt_w2 + filt_b2))
    h = jnp.dot(h, filt_w3, precision=lax.Precision.HIGHEST)
    deltas = jnp.abs(jnp.linspace(MIN_DECAY, MAX_DECAY, HY_WIDTH, dtype=F32))
    window = jnp.exp(-t[:, None] * deltas[None, :]) + DECAY_SHIFT
    return h.reshape(L, 2, 2, HY_WIDTH) * window[:, None, None, :]


def _long_conv(v, h_fwd, h_bwd, skip):
    L, C = h_fwd.shape
    k = jnp.concatenate([h_fwd, jnp.zeros((1, C), F32), h_bwd[:0:-1]], axis=0)
    vf = jnp.fft.rfft(v, n=2 * L, axis=1)
    kf = jnp.fft.rfft(k, axis=0)
    y = jnp.fft.irfft(vf * kf[None], n=2 * L, axis=1)[:, :L]
    return y + v * skip


def _hyena_mix(hy, h, hy_skip):
    x1, x2, v = jnp.split(hy, 3, axis=-1)
    z = v
    for n, gate in enumerate((x1, x2)):
        z = gate * _long_conv(z, h[:, 0, n], h[:, 1, n], hy_skip[n])
    return z


def _rope_tables(L):
    pos = jnp.arange(L, dtype=F32)
    inv = ROPE_BASE ** (-jnp.arange(0, QK_ROPE, 2, dtype=F32) / QK_ROPE)
    ang = pos[:, None] * inv[None, :]
    cos, sin = jnp.cos(ang), jnp.sin(ang)
    zeros = jnp.zeros((L, 64), F32)
    return (jnp.concatenate([cos, cos, zeros], axis=-1),
            jnp.concatenate([-sin, sin, zeros], axis=-1))


def _swap_halves(w):
    return jnp.concatenate([w[..., QK_ROPE // 2:], w[..., :QK_ROPE // 2]], axis=-1)


def _prep_weights(w_in, short_w, short_b, q_norm_g, w_uq, kv_norm_g, w_ukv, w_o_mla, w_o_hy, w_out,
                  ln1_g, ln1_b, w_ffn_up, dw_w, dw_b, w_ffn_down, ln2_g, ln2_b):
    c0, c1, c2, c3 = Q_LORA, Q_LORA + KV_LORA, Q_LORA + KV_LORA + QK_ROPE, Q_LORA + KV_LORA + QK_ROPE + IN_HY
    w_kr = w_in[:, c1:c2]
    wuq = w_uq.reshape(Q_LORA, MLA_HEADS, QK_NOPE + QK_ROPE)
    wuq_rope = wuq[..., QK_NOPE:]
    wuq = jnp.concatenate([wuq[..., :QK_NOPE], wuq_rope, _swap_halves(wuq_rope)], axis=-1)
    row = lambda a: a.reshape(1, -1)
    return dict(
        wq=w_in[:, :c0].astype(BF16), wkv=w_in[:, c0:c1].astype(BF16),
        wkr=jnp.concatenate([w_kr, _swap_halves(w_kr)], axis=-1).astype(BF16),
        why=w_in[:, c2:c3].astype(BF16), wg=w_in[:, c3:].astype(BF16),
        qg=row(q_norm_g), wuq=wuq.reshape(Q_LORA, MLA_HEADS * Q_HEAD_COLS).astype(BF16),
        kvg=row(kv_norm_g), wukv=w_ukv.astype(BF16), sw=short_w, sb=row(short_b),
        womla=w_o_mla.astype(BF16), wohy=w_o_hy.astype(BF16), wout=w_out.astype(BF16),
        ln1g=row(ln1_g), ln1b=row(ln1_b),
        wa=w_ffn_up[:, :D_FF].astype(BF16), wb=w_ffn_up[:, D_FF:].astype(BF16),
        dww=dw_w, dwb=row(dw_b), wd=w_ffn_down.astype(BF16), ln2g=row(ln2_g), ln2b=row(ln2_b),
    )


def _tiles(L):
    return dict(tm=min(256, L), tq=min(1024, L), tk=min(512, L))


def _encoder_layer(x, p, filt, hy_skip, *, tm, tq, tk):
    B, L, D = x.shape
    cs1, cs2 = _rope_tables(L)
    q, kv, kr, hy, sg = _front(x, cs1, cs2, p["wq"], p["wkv"], p["wkr"], p["why"], p["wg"], p["qg"],
                               p["wuq"], p["kvg"], p["wukv"], p["sw"], p["sb"], tm=tm)
    o = _attention(q, kv, kr, tq=tq, tk=tk)
    z = _hyena_mix(hy, _hyena_filters(L, *filt), hy_skip)
    T = B * L
    x1 = _mid(o.reshape(T, D), z.reshape(T, D), sg.reshape(T, 2 * D), x.reshape(T, D),
              p["womla"], p["wohy"], p["wout"], p["ln1g"], p["ln1b"], tm=tm)
    return _ffn(x1.reshape(B, L, D), p["wa"], p["wb"], p["dww"], p["dwb"], p["wd"],
                p["ln2g"], p["ln2b"], tm=tm)


def kernel(x_prompt, x_sample, w_in, short_w, short_b, q_norm_g, w_uq, kv_norm_g, w_ukv, w_o_mla,
           filt_w1, filt_b1, filt_freq, filt_w2, filt_b2, filt_w3, hy_skip, w_o_hy,
           w_out, ln1_g, ln1_b, w_ffn_up, dw_w, dw_b, w_ffn_down, ln2_g, ln2_b):
    y_prompt, y_sample = x_prompt, x_sample
    for i in range(DEPTH):
        p = _prep_weights(w_in[i], short_w[i], short_b[i], q_norm_g[i], w_uq[i], kv_norm_g[i], w_ukv[i],
                          w_o_mla[i], w_o_hy[i], w_out[i], ln1_g[i], ln1_b[i], w_ffn_up[i], dw_w[i],
                          dw_b[i], w_ffn_down[i], ln2_g[i], ln2_b[i])
        filt = (filt_w1[i], filt_b1[i], filt_freq[i], filt_w2[i], filt_b2[i], filt_w3[i])
        y_prompt = _encoder_layer(y_prompt, p, filt, hy_skip[i], **_tiles(y_prompt.shape[1]))
        y_sample = _encoder_layer(y_sample, p, filt, hy_skip[i], **_tiles(y_sample.shape[1]))
    return (y_prompt, y_sample)
```

```python
import functools
import math

import numpy as np
import jax
import jax.numpy as jnp
from jax import lax
from jax.experimental import pallas as pl
from jax.experimental.pallas import tpu as pltpu

F32 = jnp.float32
BF16 = jnp.bfloat16

D_MODEL = 1024
DEPTH = 1
MLA_HEADS = 8
QK_NOPE = 128
QK_ROPE = 64
V_HEAD = 128
Q_LORA = 384
KV_LORA = 256
ROPE_BASE = 10000.0
HY_WIDTH = 1024
FILT_BANDS = 16
DECAY_FAST = 0.3
DECAY_SLOW = 1.5
DECAY_TARGET = 1e-2
DECAY_SHIFT = 0.05
MAX_DECAY = math.log(DECAY_TARGET) / DECAY_FAST
MIN_DECAY = math.log(DECAY_TARGET) / DECAY_SLOW
D_FF = 2816
IN_HY = 3 * HY_WIDTH
DN_ALPHA = (2.0 * DEPTH) ** 0.25
LN_EPS = 1e-5
RMS_EPS = 1e-6

HALO = 8
Q_HEAD_COLS = 256
VMEM_LIMIT = 56 * 1024 * 1024


def _cparams(sem):
    return pltpu.CompilerParams(dimension_semantics=sem, vmem_limit_bytes=VMEM_LIMIT)


def _const_spec(shape):
    nd = len(shape)
    return pl.BlockSpec(shape, lambda *_: (0,) * nd, pipeline_mode=pl.Buffered(1))


def _dot(a, b):
    return jnp.dot(a, b, preferred_element_type=F32)


def _layer_norm(y, g, b):
    mu = jnp.mean(y, axis=-1, keepdims=True)
    yc = y - mu
    var = jnp.mean(yc * yc, axis=-1, keepdims=True)
    return yc * lax.rsqrt(var + LN_EPS) * g + b


def _rms_norm(c, g):
    return c * lax.rsqrt(jnp.mean(c * c, axis=-1, keepdims=True) + RMS_EPS) * g


def _rope128(y, cs1, cs2):
    return y * cs1 + pltpu.roll(y, 64, 1) * cs2


def _halo_rows(xp_ref, xm, xn_ref):
    i = pl.program_id(1)
    mp = (i > 0).astype(F32)
    mn = (i < pl.num_programs(1) - 1).astype(F32)
    return jnp.concatenate([xp_ref[...] * mp, xm, xn_ref[...] * mn], axis=0)


def _dwconv3_from(ue_ref, w_ref, b_ref, cols, tm):
    return (ue_ref[HALO - 1:HALO - 1 + tm, :] * w_ref[0:1, cols]
            + ue_ref[HALO:HALO + tm, :] * w_ref[1:2, cols]
            + ue_ref[HALO + 1:HALO + 1 + tm, :] * w_ref[2:3, cols]
            + b_ref[:, cols])


def _front_kernel(xp_ref, xm_ref, xn_ref, cs1_ref, cs2_ref,
                  wq_ref, wkv_ref, wkr_ref, why_ref, wg_ref,
                  qg_ref, wuq_ref, kvg_ref, wukv_ref, sw_ref, sb_ref,
                  q_ref, kv_ref, kr_ref, hy_ref, sg_ref, ue_ref, *, tm, hy_chunk):
    xm = xm_ref[...]
    xmb = xm.astype(BF16)
    cs1 = cs1_ref[...]
    cs2 = cs2_ref[...]
    scale = (QK_NOPE + QK_ROPE) ** -0.5

    cqn = _rms_norm(_dot(xmb, wq_ref[...]), qg_ref[...]).astype(BF16)
    q = _dot(cqn, wuq_ref[...])
    for h in range(MLA_HEADS):
        c0 = h * Q_HEAD_COLS
        q_ref[:, c0:c0 + QK_NOPE] = (q[:, c0:c0 + QK_NOPE] * scale).astype(BF16)
        rr = _rope128(q[:, c0 + QK_NOPE:c0 + Q_HEAD_COLS], cs1, cs2)
        q_ref[:, c0 + QK_NOPE:c0 + Q_HEAD_COLS] = (rr * scale).astype(BF16)

    ckvn = _rms_norm(_dot(xmb, wkv_ref[...]), kvg_ref[...]).astype(BF16)
    kv_ref[...] = _dot(ckvn, wukv_ref[...]).astype(BF16)

    kr_ref[...] = _rope128(_dot(xmb, wkr_ref[...]), cs1, cs2).astype(BF16)

    sg_ref[...] = jax.nn.sigmoid(_dot(xmb, wg_ref[...]))

    xe = _halo_rows(xp_ref, xm, xn_ref).astype(BF16)
    for c in range(0, IN_HY, hy_chunk):
        cols = slice(c, c + hy_chunk)
        ue_ref[...] = _dot(xe, why_ref[:, cols])
        hy_ref[:, cols] = _dwconv3_from(ue_ref, sw_ref, sb_ref, cols, tm)


def _front(x, cs1, cs2, wq, wkv, wkr, why, wg, qg, wuq, kvg, wukv, sw, sb, *, tm):
    B, L, D = x.shape
    nt = L // tm
    hb = tm // HALO
    hy_chunk = 512
    row = lambda w: pl.BlockSpec((None, tm, w), lambda b, i: (b, i, 0))
    in_specs = [
        pl.BlockSpec((None, HALO, D), lambda b, i: (b, jnp.maximum(i * hb - 1, 0), 0)),
        row(D),
        pl.BlockSpec((None, HALO, D), lambda b, i: (b, jnp.minimum((i + 1) * hb, L // HALO - 1), 0)),
        pl.BlockSpec((tm, 128), lambda b, i: (i, 0)),
        pl.BlockSpec((tm, 128), lambda b, i: (i, 0)),
    ] + [_const_spec(w.shape) for w in (wq, wkv, wkr, why, wg, qg, wuq, kvg, wukv, sw, sb)]
    out_shape = (
        jax.ShapeDtypeStruct((B, L, MLA_HEADS * Q_HEAD_COLS), BF16),
        jax.ShapeDtypeStruct((B, L, MLA_HEADS * (QK_NOPE + V_HEAD)), BF16),
        jax.ShapeDtypeStruct((B, L, 128), BF16),
        jax.ShapeDtypeStruct((B, L, IN_HY), F32),
        jax.ShapeDtypeStruct((B, L, 2 * D_MODEL), F32),
    )
    out_specs = tuple(row(s.shape[-1]) for s in out_shape)
    return pl.pallas_call(
        functools.partial(_front_kernel, tm=tm, hy_chunk=hy_chunk),
        grid=(B, nt),
        in_specs=in_specs,
        out_specs=out_specs,
        out_shape=out_shape,
        scratch_shapes=[pltpu.VMEM((tm + 2 * HALO, hy_chunk), F32)],
        compiler_params=_cparams(("parallel", "parallel")),
        name="front",
    )(x, x, x, cs1, cs2, wq, wkv, wkr, why, wg, qg, wuq, kvg, wukv, sw, sb)


def _attn_kernel(q_ref, kn_ref, kr_ref, v_ref, o_ref, m_ref, l_ref, acc_ref, *, tk, nk):
    q = q_ref[...]
    m_ref[...] = jnp.full(m_ref.shape, -jnp.inf, F32)
    l_ref[...] = jnp.zeros(l_ref.shape, F32)
    acc_ref[...] = jnp.zeros(acc_ref.shape, F32)

    def body(j, carry):
        s0 = pl.multiple_of(j * tk, tk)
        kcat = jnp.concatenate([kn_ref[pl.ds(s0, tk), :], kr_ref[pl.ds(s0, tk), :]], axis=-1)
        s = lax.dot_general(q, kcat, (((1,), (1,)), ((), ())), preferred_element_type=F32)
        m_prev = m_ref[...]
        m_new = jnp.maximum(m_prev, jnp.max(s, axis=-1, keepdims=True))
        alpha = jnp.exp(m_prev - m_new)
        p = jnp.exp(s - pltpu.repeat(m_new, tk // 128, axis=1))
        l_ref[...] = alpha * l_ref[...] + jnp.sum(p, axis=-1, keepdims=True)
        acc_ref[...] = alpha * acc_ref[...] + _dot(p.astype(BF16), v_ref[pl.ds(s0, tk), :])
        m_ref[...] = m_new
        return carry

    lax.fori_loop(0, nk, body, 0)
    o_ref[...] = (acc_ref[...] / l_ref[...]).astype(o_ref.dtype)


def _attention(q, kv, kr, *, tq, tk):
    B, L, _ = q.shape
    return pl.pallas_call(
        functools.partial(_attn_kernel, tk=tk, nk=L // tk),
        grid=(B, MLA_HEADS, L // tq),
        in_specs=[
            pl.BlockSpec((None, tq, Q_HEAD_COLS), lambda b, h, i: (b, i, h)),
            pl.BlockSpec((None, L, QK_NOPE), lambda b, h, i: (b, 0, 2 * h)),
            pl.BlockSpec((None, L, 128), lambda b, h, i: (b, 0, 0)),
            pl.BlockSpec((None, L, V_HEAD), lambda b, h, i: (b, 0, 2 * h + 1)),
        ],
        out_specs=pl.BlockSpec((None, tq, V_HEAD), lambda b, h, i: (b, i, h)),
        out_shape=jax.ShapeDtypeStruct((B, L, MLA_HEADS * V_HEAD), BF16),
        scratch_shapes=[pltpu.VMEM((tq, 128), F32), pltpu.VMEM((tq, 128), F32),
                        pltpu.VMEM((tq, V_HEAD), F32)],
        compiler_params=_cparams(("parallel", "parallel", "arbitrary")),
        name="attention",
    )(q, kv, kr, kv)


def _mid_kernel(o_ref, z_ref, sg_ref, x_ref, womla_ref, wohy_ref, wout_ref, g_ref, b_ref, out_ref):
    o_mla = _dot(o_ref[...], womla_ref[...])
    o_hy = _dot(z_ref[...].astype(BF16), wohy_ref[...])
    merged = sg_ref[:, :D_MODEL] * o_hy + sg_ref[:, D_MODEL:] * o_mla
    y = DN_ALPHA * x_ref[...] + _dot(merged.astype(BF16), wout_ref[...])
    out_ref[...] = _layer_norm(y, g_ref[...], b_ref[...])


def _mid(o, z, sg, x, womla, wohy, wout, g, b, *, tm):
    T, D = x.shape
    row = lambda w: pl.BlockSpec((tm, w), lambda i: (i, 0))
    return pl.pallas_call(
        _mid_kernel,
        grid=(T // tm,),
        in_specs=[row(D), row(D), row(2 * D), row(D)]
        + [_const_spec(w.shape) for w in (womla, wohy, wout, g, b)],
        out_specs=row(D),
        out_shape=jax.ShapeDtypeStruct((T, D), F32),
        compiler_params=_cparams(("parallel",)),
        name="mid",
    )(o, z, sg, x, womla, wohy, wout, g, b)


def _ffn_kernel(xp_ref, xm_ref, xn_ref, wa_ref, wb_ref, dww_ref, dwb_ref, wd_ref, g_ref, b_ref,
                out_ref, ae_ref, acc_ref, *, tm, ff_chunk):
    xm = xm_ref[...]
    xmb = xm.astype(BF16)
    xe = _halo_rows(xp_ref, xm, xn_ref).astype(BF16)
    for c in range(0, D_FF, ff_chunk):
        cols = slice(c, c + ff_chunk)
        ae_ref[...] = _dot(xe, wa_ref[:, cols])
        a = _dwconv3_from(ae_ref, dww_ref, dwb_ref, cols, tm)
        gelu = 0.5 * a * (1.0 + lax.erf(a * math.sqrt(0.5)))
        hmid = gelu * _dot(xmb, wb_ref[:, cols])
        part = _dot(hmid.astype(BF16), wd_ref[cols, :])
        if c == 0:
            acc_ref[...] = part
        else:
            acc_ref[...] += part
    out_ref[...] = _layer_norm(DN_ALPHA * xm + acc_ref[...], g_ref[...], b_ref[...])


def _ffn(x, wa, wb, dww, dwb, wd, g, b, *, tm):
    B, L, D = x.shape
    hb = tm // HALO
    ff_chunk = 256
    row = pl.BlockSpec((None, tm, D), lambda b_, i: (b_, i, 0))
    return pl.pallas_call(
        functools.partial(_ffn_kernel, tm=tm, ff_chunk=ff_chunk),
        grid=(B, L // tm),
        in_specs=[
            pl.BlockSpec((None, HALO, D), lambda b_, i: (b_, jnp.maximum(i * hb - 1, 0), 0)),
            row,
            pl.BlockSpec((None, HALO, D), lambda b_, i: (b_, jnp.minimum((i + 1) * hb, L // HALO - 1), 0)),
        ] + [_const_spec(w.shape) for w in (wa, wb, dww, dwb, wd, g, b)],
        out_specs=row,
        out_shape=jax.ShapeDtypeStruct((B, L, D), F32),
        scratch_shapes=[pltpu.VMEM((tm + 2 * HALO, ff_chunk), F32), pltpu.VMEM((tm, D), F32)],
        compiler_params=_cparams(("parallel", "parallel")),
        name="ffn",
    )(x, x, x, wa, wb, dww, dwb, wd, g, b)


def _hyena_filters(L, filt_w1, filt_b1, filt_freq, filt_w2, filt_b2, filt_w3):
    pos = jnp.arange(L, dtype=F32)
    t = pos / max(L - 1, 1)
    bands = jnp.linspace(1e-4, FILT_BANDS - 1, FILT_BANDS, dtype=F32)
    ang = (2.0 * math.pi * pos / L)[:, None] * bands[None, :]
    z = jnp.concatenate([t[:, None], jnp.cos(ang), -jnp.sin(ang)], axis=-1)
    h = jnp.sin(filt_freq * (z @ filt_w1 + filt_b1))
    h = jnp.sin(filt_freq * (h @ filt_w2 + filt_b2))
    h = jnp.dot(h, filt_w3, precision=lax.Precision.HIGHEST)
    deltas = jnp.abs(jnp.linspace(MIN_DECAY, MAX_DECAY, HY_WIDTH, dtype=F32))
    window = jnp.exp(-t[:, None] * deltas[None, :]) + DECAY_SHIFT
    return h.reshape(L, 2, 2, HY_WIDTH) * window[:, None, None, :]


LANE = 128
S1_ROWS = 8
S1_GROUP = 4


def _fft_factors(L):
    n = 2 * L
    n2 = 128
    return n // n2, n2


def _split3_const(w):
    w = np.asarray(w, np.float64)
    hi = jnp.asarray(w, F32).astype(BF16)
    lo = (jnp.asarray(w, F32) - hi.astype(F32)).astype(BF16)
    return jnp.concatenate([hi, lo, hi], axis=1)


def _split3(x):
    hi = x.astype(BF16)
    lo = (x - hi.astype(F32)).astype(BF16)
    return jnp.concatenate([hi, hi, lo], axis=0)


def _complex_as_real(fr, fi):
    return np.block([[fr, -fi], [fi, fr]])


def _dft_tables(L):
    n1, n2 = _fft_factors(L)
    n = n1 * n2
    k = n1 // 2
    a1 = -2.0 * np.pi * np.outer(np.arange(n1), np.arange(n1)) / n1
    f1r, f1i = np.cos(a1), np.sin(a1)
    a2 = -2.0 * np.pi * np.outer(np.arange(n2), np.arange(n2)) / n2
    f2r, f2i = np.cos(a2), np.sin(a2)
    at = -2.0 * np.pi * np.outer(np.arange(n2), np.arange(n1)) / n
    tw = lambda t: jnp.broadcast_to(jnp.asarray(t, F32)[:, :, None], (n2, n1, LANE))
    return dict(
        n1=n1, n2=n2,
        s1_pair=_split3_const(_complex_as_real(f1r[:, :k], f1i[:, :k])),
        s1_real=_split3_const(np.block([[f1r], [f1i]])),
        s1_inv=_split3_const(_complex_as_real(f1r[:k, :], -f1i[:k, :])),
        s2_fwd=_split3_const(_complex_as_real(f2r, f2i)),
        s2_inv=_split3_const(_complex_as_real(f2r, -f2i)),
        twr=tw(np.cos(at)), twi=tw(np.sin(at)),
    )


def _slot_rows(ref, half, s):
    _, rows, g, _ = ref.shape
    return ref.reshape(2 * rows * g, LANE), pl.ds((half * rows) * g + s, rows, stride=g)


def _load_slot(ref, half, s):
    flat, rows = _slot_rows(ref, half, s)
    return flat[rows, :]


def _store_slot(ref, half, s, val):
    flat, rows = _slot_rows(ref, half, s)
    flat[rows, :] = val


def _load_slots(ref, half, s0):
    return jnp.concatenate([_load_slot(ref, half, s0 + i) for i in range(S1_GROUP)], axis=1)


def _fft_s1_kernel(f_ref, twr_ref, twi_ref, z_ref, o_ref, *, n1):
    for s0 in range(0, S1_ROWS, S1_GROUP):
        x = jnp.concatenate([_load_slots(z_ref, 0, s0), _load_slots(z_ref, 1, s0)], axis=0)
        y = _dot(f_ref[...], _split3(x))
        for i in range(S1_GROUP):
            s = s0 + i
            yr = y[:n1, i * LANE:(i + 1) * LANE]
            yi = y[n1:, i * LANE:(i + 1) * LANE]
            tr, ti = twr_ref[s], twi_ref[s]
            _store_slot(o_ref, 0, s, yr * tr - yi * ti)
            _store_slot(o_ref, 1, s, yr * ti + yi * tr)


def _fft_s1(z5, col0, f, twr, twi, *, n1):
    P, _, k, n2, _ = z5.shape
    g = S1_ROWS
    ct = HY_WIDTH // LANE
    c0 = col0 // LANE
    return pl.pallas_call(
        functools.partial(_fft_s1_kernel, n1=n1),
        grid=(P, n2 // g, ct),
        in_specs=[
            _const_spec(f.shape),
            pl.BlockSpec((g, n1, LANE), lambda p, j, c: (j, 0, 0)),
            pl.BlockSpec((g, n1, LANE), lambda p, j, c: (j, 0, 0)),
            pl.BlockSpec((None, 2, k, g, LANE), lambda p, j, c: (p, 0, 0, j, c0 + c)),
        ],
        out_specs=pl.BlockSpec((None, 2, n1, g, LANE), lambda p, j, c: (p, 0, 0, j, c)),
        out_shape=jax.ShapeDtypeStruct((P, 2, n1, n2, HY_WIDTH), F32),
        compiler_params=_cparams(("parallel", "parallel", "arbitrary")),
        name="fft_s1",
    )(f, twr, twi, z5)


def _fft_mid_kernel(f_ref, g_ref, a_ref, kh_ref, o_ref, *, n2, chunk):
    for c in range(0, HY_WIDTH, chunk):
        cols = slice(c, c + chunk)
        y = _dot(f_ref[...], _split3(jnp.concatenate([a_ref[0, :, cols], a_ref[1, :, cols]], axis=0)))
        br, bi = y[:n2], y[n2:]
        kr, ki = kh_ref[0, :, cols], kh_ref[1, :, cols]
        y = _dot(g_ref[...], _split3(jnp.concatenate([br * kr - bi * ki, br * ki + bi * kr], axis=0)))
        o_ref[0, :, cols] = y[:n2]
        o_ref[1, :, cols] = y[n2:]


def _fft_mid(a, kh, f, g):
    P, _, n1, n2, C = a.shape
    slab = lambda idx: pl.BlockSpec((None, 2, None, n2, C), idx)
    return pl.pallas_call(
        functools.partial(_fft_mid_kernel, n2=n2, chunk=512),
        grid=(P, n1),
        in_specs=[_const_spec(f.shape), _const_spec(g.shape),
                  slab(lambda p, i: (p, 0, i, 0, 0)),
                  pl.BlockSpec((2, None, n2, C), lambda p, i: (0, i, 0, 0))],
        out_specs=slab(lambda p, i: (p, 0, i, 0, 0)),
        out_shape=jax.ShapeDtypeStruct(a.shape, F32),
        compiler_params=_cparams(("parallel", "parallel")),
        name="fft_mid",
    )(f, g, a, kh)


def _fft_khat_kernel(f_ref, a_ref, o_ref, *, n2, chunk, inv_n):
    for c in range(0, HY_WIDTH, chunk):
        cols = slice(c, c + chunk)
        y = _dot(f_ref[...], _split3(jnp.concatenate([a_ref[0, :, cols], a_ref[1, :, cols]], axis=0)))
        o_ref[0, :, cols] = y[:n2] * inv_n
        o_ref[1, :, cols] = y[n2:] * inv_n


def _fft_khat(a, f):
    P, _, n1, n2, C = a.shape
    slab = pl.BlockSpec((None, 2, None, n2, C), lambda p, i: (p, 0, i, 0, 0))
    return pl.pallas_call(
        functools.partial(_fft_khat_kernel, n2=n2, chunk=512, inv_n=1.0 / (n1 * n2)),
        grid=(P, n1),
        in_specs=[_const_spec(f.shape), slab],
        out_specs=slab,
        out_shape=jax.ShapeDtypeStruct(a.shape, F32),
        compiler_params=_cparams(("parallel", "parallel")),
        name="fft_khat",
    )(f, a)


def _fft_s1inv_kernel(f_ref, twr_ref, twi_ref, a_ref, gate_ref, vin_ref, skip_ref, o_ref, *, k, n1):
    skip = skip_ref[...]
    for s0 in range(0, S1_ROWS, S1_GROUP):
        ar = _load_slots(a_ref, 0, s0)
        ai = _load_slots(a_ref, 1, s0)
        tr = jnp.concatenate([twr_ref[s0 + i] for i in range(S1_GROUP)], axis=1)
        ti = jnp.concatenate([twi_ref[s0 + i] for i in range(S1_GROUP)], axis=1)
        x = jnp.concatenate([ar * tr + ai * ti, ai * tr - ar * ti], axis=0)
        y = _dot(f_ref[...], _split3(x))
        for i in range(S1_GROUP):
            s = s0 + i
            for half in range(2):
                conv = y[half * k:(half + 1) * k, i * LANE:(i + 1) * LANE]
                gated = _load_slot(gate_ref, half, s) * (conv + skip * _load_slot(vin_ref, half, s))
                _store_slot(o_ref, half, s, gated)


def _fft_s1inv(a, gate5, gate_col0, vin5, vin_col0, skip, f, twr, twi):
    P, _, n1, n2, C = a.shape
    k = n1 // 2
    g = S1_ROWS
    seq = lambda c0: pl.BlockSpec((None, 2, k, g, LANE), lambda p, j, c: (p, 0, 0, j, c0 // LANE + c))
    return pl.pallas_call(
        functools.partial(_fft_s1inv_kernel, k=k, n1=n1),
        grid=(P, n2 // g, C // LANE),
        in_specs=[
            _const_spec(f.shape),
            pl.BlockSpec((g, n1, LANE), lambda p, j, c: (j, 0, 0)),
            pl.BlockSpec((g, n1, LANE), lambda p, j, c: (j, 0, 0)),
            pl.BlockSpec((None, 2, n1, g, LANE), lambda p, j, c: (p, 0, 0, j, c)),
            seq(gate_col0), seq(vin_col0),
            pl.BlockSpec((1, LANE), lambda p, j, c: (0, c)),
        ],
        out_specs=seq(0),
        out_shape=jax.ShapeDtypeStruct((P, 2, k, n2, C), F32),
        compiler_params=_cparams(("parallel", "parallel", "arbitrary")),
        name="fft_s1inv",
    )(f, twr, twi, a, gate5, vin5, skip)


def _hyena_mix(hy, h, hy_skip):
    B, L, _ = hy.shape
    C = HY_WIDTH
    t = _dft_tables(L)
    n1, n2 = t["n1"], t["n2"]
    k = n1 // 2
    k2s = jnp.concatenate([h[:, 0], jnp.zeros((1, 2, C), F32), h[:0:-1, 1]], axis=0)
    k2s = k2s.transpose(1, 0, 2).reshape(2, 2, k, n2, C)
    khat = _fft_khat(_fft_s1(k2s, 0, t["s1_real"], t["twr"], t["twi"], n1=n1), t["s2_fwd"])
    hy5 = hy.reshape(B // 2, 2, k, n2, 3 * C)
    z5, z_col0 = hy5, 2 * C
    for order, gate_col0 in enumerate((0, C)):
        a = _fft_s1(z5, z_col0, t["s1_pair"], t["twr"], t["twi"], n1=n1)
        a = _fft_mid(a, khat[order], t["s2_fwd"], t["s2_inv"])
        z5 = _fft_s1inv(a, hy5, gate_col0, z5, z_col0, hy_skip[order:order + 1], t["s1_inv"],
                        t["twr"], t["twi"])
        z_col0 = 0
    return z5.reshape(B, L, C)


def _rope_tables(L):
    pos = jnp.arange(L, dtype=F32)
    inv = ROPE_BASE ** (-jnp.arange(0, QK_ROPE, 2, dtype=F32) / QK_ROPE)
    ang = pos[:, None] * inv[None, :]
    cos, sin = jnp.cos(ang), jnp.sin(ang)
    zeros = jnp.zeros((L, 64), F32)
    return (jnp.concatenate([cos, cos, zeros], axis=-1),
            jnp.concatenate([-sin, sin, zeros], axis=-1))


def _swap_halves(w):
    return jnp.concatenate([w[..., QK_ROPE // 2:], w[..., :QK_ROPE // 2]], axis=-1)


def _prep_weights(w_in, short_w, short_b, q_norm_g, w_uq, kv_norm_g, w_ukv, w_o_mla, w_o_hy, w_out,
                  ln1_g, ln1_b, w_ffn_up, dw_w, dw_b, w_ffn_down, ln2_g, ln2_b):
    c0, c1, c2, c3 = Q_LORA, Q_LORA + KV_LORA, Q_LORA + KV_LORA + QK_ROPE, Q_LORA + KV_LORA + QK_ROPE + IN_HY
    w_kr = w_in[:, c1:c2]
    wuq = w_uq.reshape(Q_LORA, MLA_HEADS, QK_NOPE + QK_ROPE)
    wuq_rope = wuq[..., QK_NOPE:]
    wuq = jnp.concatenate([wuq[..., :QK_NOPE], wuq_rope, _swap_halves(wuq_rope)], axis=-1)
    row = lambda a: a.reshape(1, -1)
    return dict(
        wq=w_in[:, :c0].astype(BF16), wkv=w_in[:, c0:c1].astype(BF16),
        wkr=jnp.concatenate([w_kr, _swap_halves(w_kr)], axis=-1).astype(BF16),
        why=w_in[:, c2:c3].astype(BF16), wg=w_in[:, c3:].astype(BF16),
        qg=row(q_norm_g), wuq=wuq.reshape(Q_LORA, MLA_HEADS * Q_HEAD_COLS).astype(BF16),
        kvg=row(kv_norm_g), wukv=w_ukv.astype(BF16), sw=short_w, sb=row(short_b),
        womla=w_o_mla.astype(BF16), wohy=w_o_hy.astype(BF16), wout=w_out.astype(BF16),
        ln1g=row(ln1_g), ln1b=row(ln1_b),
        wa=w_ffn_up[:, :D_FF].astype(BF16), wb=w_ffn_up[:, D_FF:].astype(BF16),
        dww=dw_w, dwb=row(dw_b), wd=w_ffn_down.astype(BF16), ln2g=row(ln2_g), ln2b=row(ln2_b),
    )


def _tiles(L):
    return dict(tm=min(256, L), tq=min(1024, L), tk=min(512, L))


def _encoder_layer(x, p, filt, hy_skip, *, tm, tq, tk):
    B, L, D = x.shape
    cs1, cs2 = _rope_tables(L)
    q, kv, kr, hy, sg = _front(x, cs1, cs2, p["wq"], p["wkv"], p["wkr"], p["why"], p["wg"], p["qg"],
                               p["wuq"], p["kvg"], p["wukv"], p["sw"], p["sb"], tm=tm)
    o = _attention(q, kv, kr, tq=tq, tk=tk)
    z = _hyena_mix(hy, _hyena_filters(L, *filt), hy_skip)
    T = B * L
    x1 = _mid(o.reshape(T, D), z.reshape(T, D), sg.reshape(T, 2 * D), x.reshape(T, D),
              p["womla"], p["wohy"], p["wout"], p["ln1g"], p["ln1b"], tm=tm)
    return _ffn(x1.reshape(B, L, D), p["wa"], p["wb"], p["dww"], p["dwb"], p["wd"],
                p["ln2g"], p["ln2b"], tm=tm)


def kernel(x_prompt, x_sample, w_in, short_w, short_b, q_norm_g, w_uq, kv_norm_g, w_ukv, w_o_mla,
           filt_w1, filt_b1, filt_freq, filt_w2, filt_b2, filt_w3, hy_skip, w_o_hy,
           w_out, ln1_g, ln1_b, w_ffn_up, dw_w, dw_b, w_ffn_down, ln2_g, ln2_b):
    y_prompt, y_sample = x_prompt, x_sample
    for i in range(DEPTH):
        p = _prep_weights(w_in[i], short_w[i], short_b[i], q_norm_g[i], w_uq[i], kv_norm_g[i], w_ukv[i],
                          w_o_mla[i], w_o_hy[i], w_out[i], ln1_g[i], ln1_b[i], w_ffn_up[i], dw_w[i],
                          dw_b[i], w_ffn_down[i], ln2_g[i], ln2_b[i])
        filt = (filt_w1[i], filt_b1[i], filt_freq[i], filt_w2[i], filt_b2[i], filt_w3[i])
        y_prompt = _encoder_layer(y_prompt, p, filt, hy_skip[i], **_tiles(y_prompt.shape[1]))
        y_sample = _encoder_layer(y_sample, p, filt, hy_skip[i], **_tiles(y_sample.shape[1]))
    return (y_prompt, y_sample)
```

```python
import functools
import math

import numpy as np
import jax
import jax.numpy as jnp
from jax import lax
from jax.experimental import pallas as pl
from jax.experimental.pallas import tpu as pltpu

F32 = jnp.float32
BF16 = jnp.bfloat16

D_MODEL = 1024
DEPTH = 1
MLA_HEADS = 8
QK_NOPE = 128
QK_ROPE = 64
V_HEAD = 128
Q_LORA = 384
KV_LORA = 256
ROPE_BASE = 10000.0
HY_WIDTH = 1024
FILT_BANDS = 16
DECAY_FAST = 0.3
DECAY_SLOW = 1.5
DECAY_TARGET = 1e-2
DECAY_SHIFT = 0.05
MAX_DECAY = math.log(DECAY_TARGET) / DECAY_FAST
MIN_DECAY = math.log(DECAY_TARGET) / DECAY_SLOW
D_FF = 2816
IN_HY = 3 * HY_WIDTH
DN_ALPHA = (2.0 * DEPTH) ** 0.25
LN_EPS = 1e-5
RMS_EPS = 1e-6

LANE = 128
HALO = 8
Q_HEAD_COLS = 256
VMEM_LIMIT = 56 * 1024 * 1024


def _cparams(sem):
    return pltpu.CompilerParams(dimension_semantics=sem, vmem_limit_bytes=VMEM_LIMIT)


def _const_spec(shape):
    nd = len(shape)
    return pl.BlockSpec(shape, lambda *_: (0,) * nd, pipeline_mode=pl.Buffered(1))


def _dot(a, b):
    return jnp.dot(a, b, preferred_element_type=F32)


def _layer_norm(y, g, b):
    mu = jnp.mean(y, axis=-1, keepdims=True)
    yc = y - mu
    var = jnp.mean(yc * yc, axis=-1, keepdims=True)
    return yc * lax.rsqrt(var + LN_EPS) * g + b


def _rms_norm(c, g):
    return c * lax.rsqrt(jnp.mean(c * c, axis=-1, keepdims=True) + RMS_EPS) * g


def _rope128(y, cs1, cs2):
    return y * cs1 + pltpu.roll(y, 64, 1) * cs2


def _halo_rows(xp_ref, xm, xn_ref):
    i = pl.program_id(1)
    mp = (i > 0).astype(F32)
    mn = (i < pl.num_programs(1) - 1).astype(F32)
    return jnp.concatenate([xp_ref[...] * mp, xm, xn_ref[...] * mn], axis=0)


def _dwconv3_from(ue_ref, w_ref, b_ref, cols, tm):
    return (ue_ref[HALO - 1:HALO - 1 + tm, :] * w_ref[0:1, cols]
            + ue_ref[HALO:HALO + tm, :] * w_ref[1:2, cols]
            + ue_ref[HALO + 1:HALO + 1 + tm, :] * w_ref[2:3, cols]
            + b_ref[:, cols])


def _front_kernel(xp_ref, xm_ref, xn_ref, cs1_ref, cs2_ref,
                  wq_ref, wkv_ref, wkr_ref, why_ref, wg_ref,
                  qg_ref, wuq_ref, kvg_ref, wukv_ref, sw_ref, sb_ref,
                  q_ref, kv_ref, kr_ref, hy_ref, sg_ref, ue_ref, *, tm, hy_chunk):
    xm = xm_ref[...]
    xmb = xm.astype(BF16)
    cs1 = cs1_ref[...]
    cs2 = cs2_ref[...]
    scale = (QK_NOPE + QK_ROPE) ** -0.5 * math.log2(math.e)

    cqn = _rms_norm(_dot(xmb, wq_ref[...]), qg_ref[...]).astype(BF16)
    q = _dot(cqn, wuq_ref[...])
    for h in range(MLA_HEADS):
        c0 = h * Q_HEAD_COLS
        q_ref[:, c0:c0 + QK_NOPE] = (q[:, c0:c0 + QK_NOPE] * scale).astype(BF16)
        rr = _rope128(q[:, c0 + QK_NOPE:c0 + Q_HEAD_COLS], cs1, cs2)
        q_ref[:, c0 + QK_NOPE:c0 + Q_HEAD_COLS] = (rr * scale).astype(BF16)

    ckvn = _rms_norm(_dot(xmb, wkv_ref[...]), kvg_ref[...]).astype(BF16)
    kv_ref[...] = _dot(ckvn, wukv_ref[...]).astype(BF16)

    kr_ref[...] = _rope128(_dot(xmb, wkr_ref[...]), cs1, cs2).astype(BF16)

    sg_ref[...] = jax.nn.sigmoid(_dot(xmb, wg_ref[...]))

    xe = _halo_rows(xp_ref, xm, xn_ref).astype(BF16)
    for c in range(0, IN_HY, hy_chunk):
        cols = slice(c, c + hy_chunk)
        ue_ref[...] = _dot(xe, why_ref[:, cols])
        hy_ref[:, cols] = _dwconv3_from(ue_ref, sw_ref, sb_ref, cols, tm)


def _front(x, cs1, cs2, wq, wkv, wkr, why, wg, qg, wuq, kvg, wukv, sw, sb, *, tm):
    B, L, D = x.shape
    nt = L // tm
    hb = tm // HALO
    hy_chunk = 512
    row = lambda w: pl.BlockSpec((None, tm, w), lambda b, i: (b, i, 0))
    in_specs = [
        pl.BlockSpec((None, HALO, D), lambda b, i: (b, jnp.maximum(i * hb - 1, 0), 0)),
        row(D),
        pl.BlockSpec((None, HALO, D), lambda b, i: (b, jnp.minimum((i + 1) * hb, L // HALO - 1), 0)),
        pl.BlockSpec((tm, 128), lambda b, i: (i, 0)),
        pl.BlockSpec((tm, 128), lambda b, i: (i, 0)),
    ] + [_const_spec(w.shape) for w in (wq, wkv, wkr, why, wg, qg, wuq, kvg, wukv, sw, sb)]
    out_shape = (
        jax.ShapeDtypeStruct((B, L, MLA_HEADS * Q_HEAD_COLS), BF16),
        jax.ShapeDtypeStruct((B, L, MLA_HEADS * (QK_NOPE + V_HEAD)), BF16),
        jax.ShapeDtypeStruct((B, L, 128), BF16),
        jax.ShapeDtypeStruct((B, L, IN_HY), F32),
        jax.ShapeDtypeStruct((B, L, 2 * D_MODEL), F32),
    )
    out_specs = tuple(row(s.shape[-1]) for s in out_shape)
    return pl.pallas_call(
        functools.partial(_front_kernel, tm=tm, hy_chunk=hy_chunk),
        grid=(B, nt),
        in_specs=in_specs,
        out_specs=out_specs,
        out_shape=out_shape,
        scratch_shapes=[pltpu.VMEM((tm + 2 * HALO, hy_chunk), F32)],
        compiler_params=_cparams(("parallel", "parallel")),
        name="front",
    )(x, x, x, cs1, cs2, wq, wkv, wkr, why, wg, qg, wuq, kvg, wukv, sw, sb)


def _attn_kernel(q_ref, kn_ref, kr_ref, v_ref, o_ref, m_ref, l_ref, acc_ref, *, tk, nk):
    q = q_ref[...]
    m_ref[...] = jnp.full(m_ref.shape, -jnp.inf, F32)
    l_ref[...] = jnp.zeros(l_ref.shape, F32)
    acc_ref[...] = jnp.zeros(acc_ref.shape, F32)

    def body(j, carry):
        s0 = pl.multiple_of(j * tk, tk)
        kcat = jnp.concatenate([kn_ref[pl.ds(s0, tk), :], kr_ref[pl.ds(s0, tk), :]], axis=-1)
        s = lax.dot_general(q, kcat, (((1,), (1,)), ((), ())), preferred_element_type=F32)
        m_prev = m_ref[...]
        m_new = jnp.maximum(m_prev, jnp.max(s, axis=-1, keepdims=True))
        alpha = jnp.exp2(m_prev - m_new)
        p = jnp.exp2(s - jnp.tile(m_new, (1, tk // LANE)))
        l_ref[...] = alpha * l_ref[...] + jnp.sum(p, axis=-1, keepdims=True)
        acc_ref[...] = alpha * acc_ref[...] + _dot(p.astype(BF16), v_ref[pl.ds(s0, tk), :])
        m_ref[...] = m_new
        return carry

    lax.fori_loop(0, nk, body, 0, unroll=min(4, nk))
    o_ref[...] = (acc_ref[...] / l_ref[...]).astype(o_ref.dtype)


def _attention(q, kv, kr, *, tq, tk):
    B, L, _ = q.shape
    return pl.pallas_call(
        functools.partial(_attn_kernel, tk=tk, nk=L // tk),
        grid=(B, MLA_HEADS, L // tq),
        in_specs=[
            pl.BlockSpec((None, tq, Q_HEAD_COLS), lambda b, h, i: (b, i, h)),
            pl.BlockSpec((None, L, QK_NOPE), lambda b, h, i: (b, 0, 2 * h)),
            pl.BlockSpec((None, L, 128), lambda b, h, i: (b, 0, 0)),
            pl.BlockSpec((None, L, V_HEAD), lambda b, h, i: (b, 0, 2 * h + 1)),
        ],
        out_specs=pl.BlockSpec((None, tq, V_HEAD), lambda b, h, i: (b, i, h)),
        out_shape=jax.ShapeDtypeStruct((B, L, MLA_HEADS * V_HEAD), BF16),
        scratch_shapes=[pltpu.VMEM((tq, 128), F32), pltpu.VMEM((tq, 128), F32),
                        pltpu.VMEM((tq, V_HEAD), F32)],
        compiler_params=_cparams(("parallel", "parallel", "arbitrary")),
        name="attention",
    )(q, kv, kr, kv)


def _mid_kernel(o_ref, z_ref, sg_ref, x_ref, womla_ref, wohy_ref, wout_ref, g_ref, b_ref, out_ref):
    o_mla = _dot(o_ref[...], womla_ref[...])
    o_hy = _dot(z_ref[...].astype(BF16), wohy_ref[...])
    merged = sg_ref[:, :D_MODEL] * o_hy + sg_ref[:, D_MODEL:] * o_mla
    y = DN_ALPHA * x_ref[...] + _dot(merged.astype(BF16), wout_ref[...])
    out_ref[...] = _layer_norm(y, g_ref[...], b_ref[...])


def _mid(o, z, sg, x, womla, wohy, wout, g, b, *, tm):
    T, D = x.shape
    row = lambda w: pl.BlockSpec((tm, w), lambda i: (i, 0))
    return pl.pallas_call(
        _mid_kernel,
        grid=(T // tm,),
        in_specs=[row(D), row(D), row(2 * D), row(D)]
        + [_const_spec(w.shape) for w in (womla, wohy, wout, g, b)],
        out_specs=row(D),
        out_shape=jax.ShapeDtypeStruct((T, D), F32),
        compiler_params=_cparams(("parallel",)),
        name="mid",
    )(o, z, sg, x, womla, wohy, wout, g, b)


def _ffn_kernel(xp_ref, xm_ref, xn_ref, wa_ref, wb_ref, dww_ref, dwb_ref, wd_ref, g_ref, b_ref,
                out_ref, ae_ref, acc_ref, *, tm, ff_chunk):
    xm = xm_ref[...]
    xmb = xm.astype(BF16)
    xe = _halo_rows(xp_ref, xm, xn_ref).astype(BF16)
    for c in range(0, D_FF, ff_chunk):
        cols = slice(c, c + ff_chunk)
        ae_ref[...] = _dot(xe, wa_ref[:, cols])
        a = _dwconv3_from(ae_ref, dww_ref, dwb_ref, cols, tm)
        gelu = 0.5 * a * (1.0 + lax.erf(a * math.sqrt(0.5)))
        hmid = gelu * _dot(xmb, wb_ref[:, cols])
        part = _dot(hmid.astype(BF16), wd_ref[cols, :])
        if c == 0:
            acc_ref[...] = part
        else:
            acc_ref[...] += part
    out_ref[...] = _layer_norm(DN_ALPHA * xm + acc_ref[...], g_ref[...], b_ref[...])


def _ffn(x, wa, wb, dww, dwb, wd, g, b, *, tm):
    B, L, D = x.shape
    hb = tm // HALO
    ff_chunk = 256
    row = pl.BlockSpec((None, tm, D), lambda b_, i: (b_, i, 0))
    return pl.pallas_call(
        functools.partial(_ffn_kernel, tm=tm, ff_chunk=ff_chunk),
        grid=(B, L // tm),
        in_specs=[
            pl.BlockSpec((None, HALO, D), lambda b_, i: (b_, jnp.maximum(i * hb - 1, 0), 0)),
            row,
            pl.BlockSpec((None, HALO, D), lambda b_, i: (b_, jnp.minimum((i + 1) * hb, L // HALO - 1), 0)),
        ] + [_const_spec(w.shape) for w in (wa, wb, dww, dwb, wd, g, b)],
        out_specs=row,
        out_shape=jax.ShapeDtypeStruct((B, L, D), F32),
        scratch_shapes=[pltpu.VMEM((tm + 2 * HALO, ff_chunk), F32), pltpu.VMEM((tm, D), F32)],
        compiler_params=_cparams(("parallel", "parallel")),
        name="ffn",
    )(x, x, x, wa, wb, dww, dwb, wd, g, b)


FILT_ROWS = 512


def _split3_rows(w):
    hi = w.astype(BF16)
    lo = (w - hi.astype(F32)).astype(BF16)
    return jnp.concatenate([hi, lo, hi], axis=0)


def _split3_lanes(x):
    hi = x.astype(BF16)
    lo = (x - hi.astype(F32)).astype(BF16)
    return jnp.concatenate([hi, hi, lo], axis=1)


def _filter_kernel(feat_ref, w1_ref, b1_ref, w2_ref, b2_ref, freq_ref, w3_ref, delta_ref, o_ref, *, L):
    feat = feat_ref[...]
    freq = freq_ref[...]
    h = jnp.sin(freq * (_dot(_split3_lanes(feat), w1_ref[...]) + b1_ref[...]))
    h = jnp.sin(freq * (_dot(_split3_lanes(h), w2_ref[...]) + b2_ref[...]))
    taps = _dot(_split3_lanes(h), w3_ref[...])
    window = jnp.exp(-feat[:, 0:1] * delta_ref[...]) + DECAY_SHIFT
    n = pl.program_id(0) * FILT_ROWS + lax.broadcasted_iota(jnp.int32, window.shape, 0)
    window = jnp.where(n == L, 0.0, window)
    for order in range(2):
        o_ref[order] = taps[:, order * HY_WIDTH:(order + 1) * HY_WIDTH] * window


def _two_sided_filters(L, filt_w1, filt_b1, filt_freq, filt_w2, filt_b2, filt_w3):
    n = jnp.arange(2 * L)
    pos = jnp.where(n <= L, n, 2 * L - n).astype(F32)
    t = pos / max(L - 1, 1)
    bands = jnp.linspace(1e-4, FILT_BANDS - 1, FILT_BANDS, dtype=F32)
    ang = (2.0 * math.pi * pos / L)[:, None] * bands[None, :]
    feat = jnp.concatenate([t[:, None], jnp.cos(ang), -jnp.sin(ang)], axis=-1)
    pad_lanes = lambda a: jnp.pad(a, [(0, 0)] * (a.ndim - 1) + [(0, LANE - a.shape[-1])])
    pad_rows = lambda a: jnp.pad(a, [(0, LANE - a.shape[0]), (0, 0)])
    feat = pad_lanes(feat)
    w1 = _split3_rows(pad_rows(pad_lanes(filt_w1)))
    w2 = _split3_rows(pad_rows(pad_lanes(filt_w2)))
    w3 = filt_w3.reshape(filt_w3.shape[0], 2, 2 * HY_WIDTH).transpose(1, 0, 2)
    w3 = jnp.stack([_split3_rows(pad_rows(w3[d])) for d in range(2)])
    row = lambda a: pad_lanes(a.reshape(1, -1))
    deltas = jnp.abs(jnp.linspace(MIN_DECAY, MAX_DECAY, HY_WIDTH, dtype=F32)).reshape(1, -1)
    steps_per_dir = L // FILT_ROWS
    return pl.pallas_call(
        functools.partial(_filter_kernel, L=L),
        grid=(2 * steps_per_dir,),
        in_specs=[
            pl.BlockSpec((FILT_ROWS, LANE), lambda i: (i, 0)),
            _const_spec(w1.shape), _const_spec((1, LANE)), _const_spec(w2.shape), _const_spec((1, LANE)),
            _const_spec((1, LANE)),
            pl.BlockSpec((None,) + w3.shape[1:], lambda i: (i // steps_per_dir, 0, 0)),
            _const_spec(deltas.shape),
        ],
        out_specs=pl.BlockSpec((2, FILT_ROWS, HY_WIDTH), lambda i: (0, i, 0)),
        out_shape=jax.ShapeDtypeStruct((2, 2 * L, HY_WIDTH), F32),
        compiler_params=_cparams(("parallel",)),
        name="filters",
    )(feat, w1, row(filt_b1), w2, row(filt_b2), row(filt_freq), w3, deltas)


S1_ROWS = 8
S1_GROUP = 4


def _fft_factors(L):
    n = 2 * L
    n2 = 128
    return n // n2, n2


def _split3_const(w):
    w = np.asarray(w, np.float64)
    hi = jnp.asarray(w, F32).astype(BF16)
    lo = (jnp.asarray(w, F32) - hi.astype(F32)).astype(BF16)
    return jnp.concatenate([hi, lo, hi], axis=1)


def _split3(x):
    hi = x.astype(BF16)
    lo = (x - hi.astype(F32)).astype(BF16)
    return jnp.concatenate([hi, hi, lo], axis=0)


def _complex_as_real(fr, fi):
    return np.block([[fr, -fi], [fi, fr]])


def _dft_tables(L):
    n1, n2 = _fft_factors(L)
    n = n1 * n2
    k = n1 // 2
    a1 = -2.0 * np.pi * np.outer(np.arange(n1), np.arange(n1)) / n1
    f1r, f1i = np.cos(a1), np.sin(a1)
    a2 = -2.0 * np.pi * np.outer(np.arange(n2), np.arange(n2)) / n2
    f2r, f2i = np.cos(a2), np.sin(a2)
    at = -2.0 * np.pi * np.outer(np.arange(n2), np.arange(n1)) / n
    tw = lambda t: jnp.broadcast_to(jnp.asarray(t, F32)[:, :, None], (n2, n1, LANE))
    return dict(
        n1=n1, n2=n2,
        s1_pair=_split3_const(_complex_as_real(f1r[:, :k], f1i[:, :k])),
        s1_real=_split3_const(np.block([[f1r], [f1i]])),
        s1_inv=_split3_const(_complex_as_real(f1r[:k, :], -f1i[:k, :])),
        s2_fwd=_split3_const(_complex_as_real(f2r, f2i)),
        s2_inv=_split3_const(_complex_as_real(f2r, -f2i)),
        twr=tw(np.cos(at)), twi=tw(np.sin(at)),
    )


def _slot_rows(ref, half, s):
    _, rows, g, _ = ref.shape
    return ref.reshape(2 * rows * g, LANE), pl.ds((half * rows) * g + s, rows, stride=g)


def _load_slot(ref, half, s):
    flat, rows = _slot_rows(ref, half, s)
    return flat[rows, :]


def _store_slot(ref, half, s, val):
    flat, rows = _slot_rows(ref, half, s)
    flat[rows, :] = val


def _load_slots(ref, half, s0):
    return jnp.concatenate([_load_slot(ref, half, s0 + i) for i in range(S1_GROUP)], axis=1)


def _fft_s1_kernel(f_ref, twr_ref, twi_ref, z_ref, o_ref, *, n1):
    for s0 in range(0, S1_ROWS, S1_GROUP):
        x = jnp.concatenate([_load_slots(z_ref, 0, s0), _load_slots(z_ref, 1, s0)], axis=0)
        y = _dot(f_ref[...], _split3(x))
        for i in range(S1_GROUP):
            s = s0 + i
            yr = y[:n1, i * LANE:(i + 1) * LANE]
            yi = y[n1:, i * LANE:(i + 1) * LANE]
            tr, ti = twr_ref[s], twi_ref[s]
            _store_slot(o_ref, 0, s, yr * tr - yi * ti)
            _store_slot(o_ref, 1, s, yr * ti + yi * tr)


def _fft_s1(z5, col0, f, twr, twi, *, n1):
    P, _, k, n2, _ = z5.shape
    g = S1_ROWS
    ct = HY_WIDTH // LANE
    c0 = col0 // LANE
    return pl.pallas_call(
        functools.partial(_fft_s1_kernel, n1=n1),
        grid=(P, n2 // g, ct),
        in_specs=[
            _const_spec(f.shape),
            pl.BlockSpec((g, n1, LANE), lambda p, j, c: (j, 0, 0)),
            pl.BlockSpec((g, n1, LANE), lambda p, j, c: (j, 0, 0)),
            pl.BlockSpec((None, 2, k, g, LANE), lambda p, j, c: (p, 0, 0, j, c0 + c)),
        ],
        out_specs=pl.BlockSpec((None, 2, n1, g, LANE), lambda p, j, c: (p, 0, 0, j, c)),
        out_shape=jax.ShapeDtypeStruct((P, 2, n1, n2, HY_WIDTH), F32),
        compiler_params=_cparams(("parallel", "parallel", "arbitrary")),
        name="fft_s1",
    )(f, twr, twi, z5)


def _fft_mid_kernel(f_ref, g_ref, a_ref, kh_ref, o_ref, *, n2, chunk):
    for c in range(0, HY_WIDTH, chunk):
        cols = slice(c, c + chunk)
        y = _dot(f_ref[...], _split3(jnp.concatenate([a_ref[0, :, cols], a_ref[1, :, cols]], axis=0)))
        br, bi = y[:n2], y[n2:]
        kr, ki = kh_ref[0, :, cols], kh_ref[1, :, cols]
        y = _dot(g_ref[...], _split3(jnp.concatenate([br * kr - bi * ki, br * ki + bi * kr], axis=0)))
        o_ref[0, :, cols] = y[:n2]
        o_ref[1, :, cols] = y[n2:]


def _fft_mid(a, kh, f, g):
    P, _, n1, n2, C = a.shape
    slab = lambda idx: pl.BlockSpec((None, 2, None, n2, C), idx)
    return pl.pallas_call(
        functools.partial(_fft_mid_kernel, n2=n2, chunk=512),
        grid=(P, n1),
        in_specs=[_const_spec(f.shape), _const_spec(g.shape),
                  slab(lambda p, i: (p, 0, i, 0, 0)),
                  pl.BlockSpec((2, None, n2, C), lambda p, i: (0, i, 0, 0))],
        out_specs=slab(lambda p, i: (p, 0, i, 0, 0)),
        out_shape=jax.ShapeDtypeStruct(a.shape, F32),
        compiler_params=_cparams(("parallel", "parallel")),
        name="fft_mid",
    )(f, g, a, kh)


def _fft_khat_kernel(f_ref, a_ref, o_ref, *, n2, chunk, inv_n):
    for c in range(0, HY_WIDTH, chunk):
        cols = slice(c, c + chunk)
        y = _dot(f_ref[...], _split3(jnp.concatenate([a_ref[0, :, cols], a_ref[1, :, cols]], axis=0)))
        o_ref[0, :, cols] = y[:n2] * inv_n
        o_ref[1, :, cols] = y[n2:] * inv_n


def _fft_khat(a, f):
    P, _, n1, n2, C = a.shape
    slab = pl.BlockSpec((None, 2, None, n2, C), lambda p, i: (p, 0, i, 0, 0))
    return pl.pallas_call(
        functools.partial(_fft_khat_kernel, n2=n2, chunk=512, inv_n=1.0 / (n1 * n2)),
        grid=(P, n1),
        in_specs=[_const_spec(f.shape), slab],
        out_specs=slab,
        out_shape=jax.ShapeDtypeStruct(a.shape, F32),
        compiler_params=_cparams(("parallel", "parallel")),
        name="fft_khat",
    )(f, a)


def _fft_s1inv_kernel(f_ref, twr_ref, twi_ref, a_ref, gate_ref, vin_ref, skip_ref, o_ref, *, k, n1):
    skip = skip_ref[...]
    for s0 in range(0, S1_ROWS, S1_GROUP):
        ar = _load_slots(a_ref, 0, s0)
        ai = _load_slots(a_ref, 1, s0)
        tr = jnp.concatenate([twr_ref[s0 + i] for i in range(S1_GROUP)], axis=1)
        ti = jnp.concatenate([twi_ref[s0 + i] for i in range(S1_GROUP)], axis=1)
        x = jnp.concatenate([ar * tr + ai * ti, ai * tr - ar * ti], axis=0)
        y = _dot(f_ref[...], _split3(x))
        for i in range(S1_GROUP):
            s = s0 + i
            for half in range(2):
                conv = y[half * k:(half + 1) * k, i * LANE:(i + 1) * LANE]
                gated = _load_slot(gate_ref, half, s) * (conv + skip * _load_slot(vin_ref, half, s))
                _store_slot(o_ref, half, s, gated)


def _fft_s1inv(a, gate5, gate_col0, vin5, vin_col0, skip, f, twr, twi):
    P, _, n1, n2, C = a.shape
    k = n1 // 2
    g = S1_ROWS
    seq = lambda c0: pl.BlockSpec((None, 2, k, g, LANE), lambda p, j, c: (p, 0, 0, j, c0 // LANE + c))
    return pl.pallas_call(
        functools.partial(_fft_s1inv_kernel, k=k, n1=n1),
        grid=(P, n2 // g, C // LANE),
        in_specs=[
            _const_spec(f.shape),
            pl.BlockSpec((g, n1, LANE), lambda p, j, c: (j, 0, 0)),
            pl.BlockSpec((g, n1, LANE), lambda p, j, c: (j, 0, 0)),
            pl.BlockSpec((None, 2, n1, g, LANE), lambda p, j, c: (p, 0, 0, j, c)),
            seq(gate_col0), seq(vin_col0),
            pl.BlockSpec((1, LANE), lambda p, j, c: (0, c)),
        ],
        out_specs=seq(0),
        out_shape=jax.ShapeDtypeStruct((P, 2, k, n2, C), F32),
        compiler_params=_cparams(("parallel", "parallel", "arbitrary")),
        name="fft_s1inv",
    )(f, twr, twi, a, gate5, vin5, skip)


def _hyena_mix(hy, k2s, hy_skip):
    B, L, _ = hy.shape
    C = HY_WIDTH
    t = _dft_tables(L)
    n1, n2 = t["n1"], t["n2"]
    k = n1 // 2
    k2s = k2s.reshape(2, 2, k, n2, C)
    khat = _fft_khat(_fft_s1(k2s, 0, t["s1_real"], t["twr"], t["twi"], n1=n1), t["s2_fwd"])
    hy5 = hy.reshape(B // 2, 2, k, n2, 3 * C)
    z5, z_col0 = hy5, 2 * C
    for order, gate_col0 in enumerate((0, C)):
        a = _fft_s1(z5, z_col0, t["s1_pair"], t["twr"], t["twi"], n1=n1)
        a = _fft_mid(a, khat[order], t["s2_fwd"], t["s2_inv"])
        z5 = _fft_s1inv(a, hy5, gate_col0, z5, z_col0, hy_skip[order:order + 1], t["s1_inv"],
                        t["twr"], t["twi"])
        z_col0 = 0
    return z5.reshape(B, L, C)


def _rope_tables(L):
    pos = jnp.arange(L, dtype=F32)
    inv = ROPE_BASE ** (-jnp.arange(0, QK_ROPE, 2, dtype=F32) / QK_ROPE)
    ang = pos[:, None] * inv[None, :]
    cos, sin = jnp.cos(ang), jnp.sin(ang)
    zeros = jnp.zeros((L, 64), F32)
    return (jnp.concatenate([cos, cos, zeros], axis=-1),
            jnp.concatenate([-sin, sin, zeros], axis=-1))


def _swap_halves(w):
    return jnp.concatenate([w[..., QK_ROPE // 2:], w[..., :QK_ROPE // 2]], axis=-1)


def _prep_weights(w_in, short_w, short_b, q_norm_g, w_uq, kv_norm_g, w_ukv, w_o_mla, w_o_hy, w_out,
                  ln1_g, ln1_b, w_ffn_up, dw_w, dw_b, w_ffn_down, ln2_g, ln2_b):
    c0, c1, c2, c3 = Q_LORA, Q_LORA + KV_LORA, Q_LORA + KV_LORA + QK_ROPE, Q_LORA + KV_LORA + QK_ROPE + IN_HY
    w_kr = w_in[:, c1:c2]
    wuq = w_uq.reshape(Q_LORA, MLA_HEADS, QK_NOPE + QK_ROPE)
    wuq_rope = wuq[..., QK_NOPE:]
    wuq = jnp.concatenate([wuq[..., :QK_NOPE], wuq_rope, _swap_halves(wuq_rope)], axis=-1)
    row = lambda a: a.reshape(1, -1)
    return dict(
        wq=w_in[:, :c0].astype(BF16), wkv=w_in[:, c0:c1].astype(BF16),
        wkr=jnp.concatenate([w_kr, _swap_halves(w_kr)], axis=-1).astype(BF16),
        why=w_in[:, c2:c3].astype(BF16), wg=w_in[:, c3:].astype(BF16),
        qg=row(q_norm_g), wuq=wuq.reshape(Q_LORA, MLA_HEADS * Q_HEAD_COLS).astype(BF16),
        kvg=row(kv_norm_g), wukv=w_ukv.astype(BF16), sw=short_w, sb=row(short_b),
        womla=w_o_mla.astype(BF16), wohy=w_o_hy.astype(BF16), wout=w_out.astype(BF16),
        ln1g=row(ln1_g), ln1b=row(ln1_b),
        wa=w_ffn_up[:, :D_FF].astype(BF16), wb=w_ffn_up[:, D_FF:].astype(BF16),
        dww=dw_w, dwb=row(dw_b), wd=w_ffn_down.astype(BF16), ln2g=row(ln2_g), ln2b=row(ln2_b),
    )


def _tiles(L):
    return dict(tm=min(256, L), tq=min(1024, L), tk=min(512, L))


def _encoder_layer(x, p, filt, hy_skip, *, tm, tq, tk):
    B, L, D = x.shape
    cs1, cs2 = _rope_tables(L)
    q, kv, kr, hy, sg = _front(x, cs1, cs2, p["wq"], p["wkv"], p["wkr"], p["why"], p["wg"], p["qg"],
                               p["wuq"], p["kvg"], p["wukv"], p["sw"], p["sb"], tm=tm)
    o = _attention(q, kv, kr, tq=tq, tk=tk)
    z = _hyena_mix(hy, _two_sided_filters(L, *filt), hy_skip)
    T = B * L
    x1 = _mid(o.reshape(T, D), z.reshape(T, D), sg.reshape(T, 2 * D), x.reshape(T, D),
              p["womla"], p["wohy"], p["wout"], p["ln1g"], p["ln1b"], tm=tm)
    return _ffn(x1.reshape(B, L, D), p["wa"], p["wb"], p["dww"], p["dwb"], p["wd"],
                p["ln2g"], p["ln2b"], tm=tm)


def kernel(x_prompt, x_sample, w_in, short_w, short_b, q_norm_g, w_uq, kv_norm_g, w_ukv, w_o_mla,
           filt_w1, filt_b1, filt_freq, filt_w2, filt_b2, filt_w3, hy_skip, w_o_hy,
           w_out, ln1_g, ln1_b, w_ffn_up, dw_w, dw_b, w_ffn_down, ln2_g, ln2_b):
    y_prompt, y_sample = x_prompt, x_sample
    for i in range(DEPTH):
        p = _prep_weights(w_in[i], short_w[i], short_b[i], q_norm_g[i], w_uq[i], kv_norm_g[i], w_ukv[i],
                          w_o_mla[i], w_o_hy[i], w_out[i], ln1_g[i], ln1_b[i], w_ffn_up[i], dw_w[i],
                          dw_b[i], w_ffn_down[i], ln2_g[i], ln2_b[i])
        filt = (filt_w1[i], filt_b1[i], filt_freq[i], filt_w2[i], filt_b2[i], filt_w3[i])
        y_prompt = _encoder_layer(y_prompt, p, filt, hy_skip[i], **_tiles(y_prompt.shape[1]))
        y_sample = _encoder_layer(y_sample, p, filt, hy_skip[i], **_tiles(y_sample.shape[1]))
    return (y_prompt, y_sample)
```

```python
import functools
import math

import numpy as np
import jax
import jax.numpy as jnp
from jax import lax
from jax.experimental import pallas as pl
from jax.experimental.pallas import tpu as pltpu

F32 = jnp.float32
BF16 = jnp.bfloat16

D_MODEL = 1024
DEPTH = 1
MLA_HEADS = 8
QK_NOPE = 128
QK_ROPE = 64
V_HEAD = 128
Q_LORA = 384
KV_LORA = 256
ROPE_BASE = 10000.0
HY_WIDTH = 1024
FILT_BANDS = 16
DECAY_FAST = 0.3
DECAY_SLOW = 1.5
DECAY_TARGET = 1e-2
DECAY_SHIFT = 0.05
MAX_DECAY = math.log(DECAY_TARGET) / DECAY_FAST
MIN_DECAY = math.log(DECAY_TARGET) / DECAY_SLOW
D_FF = 2816
IN_HY = 3 * HY_WIDTH
DN_ALPHA = (2.0 * DEPTH) ** 0.25
LN_EPS = 1e-5
RMS_EPS = 1e-6

LANE = 128
HALO = 8
Q_HEAD_COLS = 256
VMEM_LIMIT = 56 * 1024 * 1024


def _cparams(sem):
    return pltpu.CompilerParams(dimension_semantics=sem, vmem_limit_bytes=VMEM_LIMIT)


def _const_spec(shape):
    nd = len(shape)
    return pl.BlockSpec(shape, lambda *_: (0,) * nd, pipeline_mode=pl.Buffered(1))


def _dot(a, b):
    return jnp.dot(a, b, preferred_element_type=F32)


def _layer_norm(y, g, b):
    mu = jnp.mean(y, axis=-1, keepdims=True)
    yc = y - mu
    var = jnp.mean(yc * yc, axis=-1, keepdims=True)
    return yc * lax.rsqrt(var + LN_EPS) * g + b


def _rms_norm(c, g):
    return c * lax.rsqrt(jnp.mean(c * c, axis=-1, keepdims=True) + RMS_EPS) * g


def _rope128(y, cs1, cs2):
    return y * cs1 + pltpu.roll(y, 64, 1) * cs2


def _halo_rows(xp_ref, xm, xn_ref):
    i = pl.program_id(1)
    mp = (i > 0).astype(F32)
    mn = (i < pl.num_programs(1) - 1).astype(F32)
    return jnp.concatenate([xp_ref[...] * mp, xm, xn_ref[...] * mn], axis=0)


def _dwconv3_from(ue_ref, w_ref, b_ref, cols, tm):
    return (ue_ref[HALO - 1:HALO - 1 + tm, :] * w_ref[0:1, cols]
            + ue_ref[HALO:HALO + tm, :] * w_ref[1:2, cols]
            + ue_ref[HALO + 1:HALO + 1 + tm, :] * w_ref[2:3, cols]
            + b_ref[:, cols])


def _front_kernel(xp_ref, xm_ref, xn_ref, cs1_ref, cs2_ref,
                  wq_ref, wkv_ref, wkr_ref, why_ref, wg_ref,
                  qg_ref, wuq_ref, kvg_ref, wukv_ref, sw_ref, sb_ref,
                  q_ref, kv_ref, kr_ref, hy_ref, sg_ref, ue_ref, *, tm, hy_chunk):
    xm = xm_ref[...]
    xmb = xm.astype(BF16)
    cs1 = cs1_ref[...]
    cs2 = cs2_ref[...]
    scale = (QK_NOPE + QK_ROPE) ** -0.5 * math.log2(math.e)

    cqn = _rms_norm(_dot(xmb, wq_ref[...]), qg_ref[...]).astype(BF16)
    q = _dot(cqn, wuq_ref[...])
    for h in range(MLA_HEADS):
        c0 = h * Q_HEAD_COLS
        q_ref[:, c0:c0 + QK_NOPE] = (q[:, c0:c0 + QK_NOPE] * scale).astype(BF16)
        rr = _rope128(q[:, c0 + QK_NOPE:c0 + Q_HEAD_COLS], cs1, cs2)
        q_ref[:, c0 + QK_NOPE:c0 + Q_HEAD_COLS] = (rr * scale).astype(BF16)

    ckvn = _rms_norm(_dot(xmb, wkv_ref[...]), kvg_ref[...]).astype(BF16)
    kv_ref[...] = _dot(ckvn, wukv_ref[...]).astype(BF16)

    kr_ref[...] = _rope128(_dot(xmb, wkr_ref[...]), cs1, cs2).astype(BF16)

    sg_ref[...] = jax.nn.sigmoid(_dot(xmb, wg_ref[...]))

    xe = _halo_rows(xp_ref, xm, xn_ref).astype(BF16)
    for c in range(0, IN_HY, hy_chunk):
        cols = slice(c, c + hy_chunk)
        ue_ref[...] = _dot(xe, why_ref[:, cols])
        hy_ref[:, cols] = _dwconv3_from(ue_ref, sw_ref, sb_ref, cols, tm)


def _front(x, cs1, cs2, wq, wkv, wkr, why, wg, qg, wuq, kvg, wukv, sw, sb, *, tm):
    B, L, D = x.shape
    nt = L // tm
    hb = tm // HALO
    hy_chunk = 512
    row = lambda w: pl.BlockSpec((None, tm, w), lambda b, i: (b, i, 0))
    in_specs = [
        pl.BlockSpec((None, HALO, D), lambda b, i: (b, jnp.maximum(i * hb - 1, 0), 0)),
        row(D),
        pl.BlockSpec((None, HALO, D), lambda b, i: (b, jnp.minimum((i + 1) * hb, L // HALO - 1), 0)),
        pl.BlockSpec((tm, 128), lambda b, i: (i, 0)),
        pl.BlockSpec((tm, 128), lambda b, i: (i, 0)),
    ] + [_const_spec(w.shape) for w in (wq, wkv, wkr, why, wg, qg, wuq, kvg, wukv, sw, sb)]
    out_shape = (
        jax.ShapeDtypeStruct((B, L, MLA_HEADS * Q_HEAD_COLS), BF16),
        jax.ShapeDtypeStruct((B, L, MLA_HEADS * (QK_NOPE + V_HEAD)), BF16),
        jax.ShapeDtypeStruct((B, L, 128), BF16),
        jax.ShapeDtypeStruct((B, L, IN_HY), F32),
        jax.ShapeDtypeStruct((B, L, 2 * D_MODEL), F32),
    )
    out_specs = tuple(row(s.shape[-1]) for s in out_shape)
    return pl.pallas_call(
        functools.partial(_front_kernel, tm=tm, hy_chunk=hy_chunk),
        grid=(B, nt),
        in_specs=in_specs,
        out_specs=out_specs,
        out_shape=out_shape,
        scratch_shapes=[pltpu.VMEM((tm + 2 * HALO, hy_chunk), F32)],
        compiler_params=_cparams(("parallel", "parallel")),
        name="front",
    )(x, x, x, cs1, cs2, wq, wkv, wkr, why, wg, qg, wuq, kvg, wukv, sw, sb)


def _attn_kernel(q_ref, kn_ref, kr_ref, v_ref, o_ref, m_ref, l_ref, acc_ref, *, tk, nk):
    q = q_ref[...]
    m_ref[...] = jnp.full(m_ref.shape, -jnp.inf, F32)
    l_ref[...] = jnp.zeros(l_ref.shape, F32)
    acc_ref[...] = jnp.zeros(acc_ref.shape, F32)

    def body(j, carry):
        s0 = pl.multiple_of(j * tk, tk)
        kcat = jnp.concatenate([kn_ref[pl.ds(s0, tk), :], kr_ref[pl.ds(s0, tk), :]], axis=-1)
        s = lax.dot_general(q, kcat, (((1,), (1,)), ((), ())), preferred_element_type=F32)
        m_prev = m_ref[...]
        m_new = jnp.maximum(m_prev, jnp.max(s, axis=-1, keepdims=True))
        alpha = jnp.exp2(m_prev - m_new)
        p = jnp.exp2(s - jnp.tile(m_new, (1, tk // LANE)))
        l_ref[...] = alpha * l_ref[...] + jnp.sum(p, axis=-1, keepdims=True)
        acc_ref[...] = alpha * acc_ref[...] + _dot(p.astype(BF16), v_ref[pl.ds(s0, tk), :])
        m_ref[...] = m_new
        return carry

    lax.fori_loop(0, nk, body, 0, unroll=min(4, nk))
    o_ref[...] = (acc_ref[...] / l_ref[...]).astype(o_ref.dtype)


def _attention(q, kv, kr, *, tq, tk):
    B, L, _ = q.shape
    return pl.pallas_call(
        functools.partial(_attn_kernel, tk=tk, nk=L // tk),
        grid=(B, MLA_HEADS, L // tq),
        in_specs=[
            pl.BlockSpec((None, tq, Q_HEAD_COLS), lambda b, h, i: (b, i, h)),
            pl.BlockSpec((None, L, QK_NOPE), lambda b, h, i: (b, 0, 2 * h)),
            pl.BlockSpec((None, L, 128), lambda b, h, i: (b, 0, 0)),
            pl.BlockSpec((None, L, V_HEAD), lambda b, h, i: (b, 0, 2 * h + 1)),
        ],
        out_specs=pl.BlockSpec((None, tq, V_HEAD), lambda b, h, i: (b, i, h)),
        out_shape=jax.ShapeDtypeStruct((B, L, MLA_HEADS * V_HEAD), BF16),
        scratch_shapes=[pltpu.VMEM((tq, 128), F32), pltpu.VMEM((tq, 128), F32),
                        pltpu.VMEM((tq, V_HEAD), F32)],
        compiler_params=_cparams(("parallel", "parallel", "arbitrary")),
        name="attention",
    )(q, kv, kr, kv)


def _mid_kernel(o_ref, z_ref, sg_ref, x_ref, womla_ref, wohy_ref, wout_ref, g_ref, b_ref, out_ref):
    o_mla = _dot(o_ref[...], womla_ref[...])
    o_hy = _dot(z_ref[...].astype(BF16), wohy_ref[...])
    merged = sg_ref[:, :D_MODEL] * o_hy + sg_ref[:, D_MODEL:] * o_mla
    y = DN_ALPHA * x_ref[...] + _dot(merged.astype(BF16), wout_ref[...])
    out_ref[...] = _layer_norm(y, g_ref[...], b_ref[...])


def _mid(o, z, sg, x, womla, wohy, wout, g, b, *, tm):
    T, D = x.shape
    row = lambda w: pl.BlockSpec((tm, w), lambda i: (i, 0))
    return pl.pallas_call(
        _mid_kernel,
        grid=(T // tm,),
        in_specs=[row(D), row(D), row(2 * D), row(D)]
        + [_const_spec(w.shape) for w in (womla, wohy, wout, g, b)],
        out_specs=row(D),
        out_shape=jax.ShapeDtypeStruct((T, D), F32),
        compiler_params=_cparams(("parallel",)),
        name="mid",
    )(o, z, sg, x, womla, wohy, wout, g, b)


def _ffn_kernel(xp_ref, xm_ref, xn_ref, wa_ref, wb_ref, dww_ref, dwb_ref, wd_ref, g_ref, b_ref,
                out_ref, ae_ref, acc_ref, *, tm, ff_chunk):
    xm = xm_ref[...]
    xmb = xm.astype(BF16)
    xe = _halo_rows(xp_ref, xm, xn_ref).astype(BF16)
    for c in range(0, D_FF, ff_chunk):
        cols = slice(c, c + ff_chunk)
        ae_ref[...] = _dot(xe, wa_ref[:, cols])
        a = _dwconv3_from(ae_ref, dww_ref, dwb_ref, cols, tm)
        gelu = 0.5 * a * (1.0 + lax.erf(a * math.sqrt(0.5)))
        hmid = gelu * _dot(xmb, wb_ref[:, cols])
        part = _dot(hmid.astype(BF16), wd_ref[cols, :])
        if c == 0:
            acc_ref[...] = part
        else:
            acc_ref[...] += part
    out_ref[...] = _layer_norm(DN_ALPHA * xm + acc_ref[...], g_ref[...], b_ref[...])


def _ffn(x, wa, wb, dww, dwb, wd, g, b, *, tm):
    B, L, D = x.shape
    hb = tm // HALO
    ff_chunk = 256
    row = pl.BlockSpec((None, tm, D), lambda b_, i: (b_, i, 0))
    return pl.pallas_call(
        functools.partial(_ffn_kernel, tm=tm, ff_chunk=ff_chunk),
        grid=(B, L // tm),
        in_specs=[
            pl.BlockSpec((None, HALO, D), lambda b_, i: (b_, jnp.maximum(i * hb - 1, 0), 0)),
            row,
            pl.BlockSpec((None, HALO, D), lambda b_, i: (b_, jnp.minimum((i + 1) * hb, L // HALO - 1), 0)),
        ] + [_const_spec(w.shape) for w in (wa, wb, dww, dwb, wd, g, b)],
        out_specs=row,
        out_shape=jax.ShapeDtypeStruct((B, L, D), F32),
        scratch_shapes=[pltpu.VMEM((tm + 2 * HALO, ff_chunk), F32), pltpu.VMEM((tm, D), F32)],
        compiler_params=_cparams(("parallel", "parallel")),
        name="ffn",
    )(x, x, x, wa, wb, dww, dwb, wd, g, b)


FILT_ROWS = 512


def _split3_rows(w):
    hi = w.astype(BF16)
    lo = (w - hi.astype(F32)).astype(BF16)
    return jnp.concatenate([hi, lo, hi], axis=0)


def _split3_lanes(x):
    hi = x.astype(BF16)
    lo = (x - hi.astype(F32)).astype(BF16)
    return jnp.concatenate([hi, hi, lo], axis=1)


def _filter_kernel(feat_ref, w1_ref, b1_ref, w2_ref, b2_ref, freq_ref, w3_ref, delta_ref, o_ref, *, L):
    feat = feat_ref[...]
    freq = freq_ref[...]
    h = jnp.sin(freq * (_dot(_split3_lanes(feat), w1_ref[...]) + b1_ref[...]))
    h = jnp.sin(freq * (_dot(_split3_lanes(h), w2_ref[...]) + b2_ref[...]))
    taps = _dot(_split3_lanes(h), w3_ref[...])
    window = jnp.exp(-feat[:, 0:1] * delta_ref[...]) + DECAY_SHIFT
    n = pl.program_id(0) * FILT_ROWS + lax.broadcasted_iota(jnp.int32, window.shape, 0)
    window = jnp.where(n == L, 0.0, window)
    for order in range(2):
        o_ref[order] = taps[:, order * HY_WIDTH:(order + 1) * HY_WIDTH] * window


def _two_sided_filters(L, filt_w1, filt_b1, filt_freq, filt_w2, filt_b2, filt_w3):
    n = jnp.arange(2 * L)
    pos = jnp.where(n <= L, n, 2 * L - n).astype(F32)
    t = pos / max(L - 1, 1)
    bands = jnp.linspace(1e-4, FILT_BANDS - 1, FILT_BANDS, dtype=F32)
    ang = (2.0 * math.pi * pos / L)[:, None] * bands[None, :]
    feat = jnp.concatenate([t[:, None], jnp.cos(ang), -jnp.sin(ang)], axis=-1)
    pad_lanes = lambda a: jnp.pad(a, [(0, 0)] * (a.ndim - 1) + [(0, LANE - a.shape[-1])])
    pad_rows = lambda a: jnp.pad(a, [(0, LANE - a.shape[0]), (0, 0)])
    feat = pad_lanes(feat)
    w1 = _split3_rows(pad_rows(pad_lanes(filt_w1)))
    w2 = _split3_rows(pad_rows(pad_lanes(filt_w2)))
    w3 = filt_w3.reshape(filt_w3.shape[0], 2, 2 * HY_WIDTH).transpose(1, 0, 2)
    w3 = jnp.stack([_split3_rows(pad_rows(w3[d])) for d in range(2)])
    row = lambda a: pad_lanes(a.reshape(1, -1))
    deltas = jnp.abs(jnp.linspace(MIN_DECAY, MAX_DECAY, HY_WIDTH, dtype=F32)).reshape(1, -1)
    steps_per_dir = L // FILT_ROWS
    return pl.pallas_call(
        functools.partial(_filter_kernel, L=L),
        grid=(2 * steps_per_dir,),
        in_specs=[
            pl.BlockSpec((FILT_ROWS, LANE), lambda i: (i, 0)),
            _const_spec(w1.shape), _const_spec((1, LANE)), _const_spec(w2.shape), _const_spec((1, LANE)),
            _const_spec((1, LANE)),
            pl.BlockSpec((None,) + w3.shape[1:], lambda i: (i // steps_per_dir, 0, 0)),
            _const_spec(deltas.shape),
        ],
        out_specs=pl.BlockSpec((2, FILT_ROWS, HY_WIDTH), lambda i: (0, i, 0)),
        out_shape=jax.ShapeDtypeStruct((2, 2 * L, HY_WIDTH), F32),
        compiler_params=_cparams(("parallel",)),
        name="filters",
    )(feat, w1, row(filt_b1), w2, row(filt_b2), row(filt_freq), w3, deltas)


S1_ROWS = 8
S1_GROUP = 4
MID_SLABS = 2
CONV_PASSES = 1


def _fft_factors(L):
    n = 2 * L
    n2 = 128
    return n // n2, n2


def _dft_const(w, passes):
    w = jnp.asarray(np.asarray(w, np.float64), F32)
    hi = w.astype(BF16)
    if passes == 1:
        return hi
    lo = (w - hi.astype(F32)).astype(BF16)
    return jnp.concatenate([hi, lo, hi], axis=1)


def _dft_dot(f_ref, x):
    hi = x.astype(BF16)
    if f_ref.shape[1] == x.shape[0]:
        return _dot(f_ref[...], hi)
    lo = (x - hi.astype(F32)).astype(BF16)
    return _dot(f_ref[...], jnp.concatenate([hi, hi, lo], axis=0))


def _complex_as_real(fr, fi):
    return np.block([[fr, -fi], [fi, fr]])


def _dft_tables(L):
    n1, n2 = _fft_factors(L)
    n = n1 * n2
    k = n1 // 2
    a1 = -2.0 * np.pi * np.outer(np.arange(n1), np.arange(n1)) / n1
    f1r, f1i = np.cos(a1), np.sin(a1)
    a2 = -2.0 * np.pi * np.outer(np.arange(n2), np.arange(n2)) / n2
    f2r, f2i = np.cos(a2), np.sin(a2)
    at = -2.0 * np.pi * np.outer(np.arange(n2), np.arange(n1)) / n
    tw = lambda t: jnp.broadcast_to(jnp.asarray(t, F32)[:, :, None], (n2, n1, LANE))
    s2_fwd = _complex_as_real(f2r, f2i)
    return dict(
        n1=n1, n2=n2,
        s1_real=_dft_const(np.block([[f1r], [f1i]]), 3),
        s2_filt=_dft_const(s2_fwd, 3),
        s1_pair=_dft_const(_complex_as_real(f1r[:, :k], f1i[:, :k]), CONV_PASSES),
        s1_inv=_dft_const(_complex_as_real(f1r[:k, :], -f1i[:k, :]), CONV_PASSES),
        s2_fwd=_dft_const(s2_fwd, CONV_PASSES),
        s2_inv=_dft_const(_complex_as_real(f2r, -f2i), CONV_PASSES),
        twr=tw(np.cos(at)), twi=tw(np.sin(at)),
    )


def _slot_rows(ref, half, s):
    _, rows, g, _ = ref.shape
    return ref.reshape(2 * rows * g, LANE), pl.ds((half * rows) * g + s, rows, stride=g)


def _load_slot(ref, half, s):
    flat, rows = _slot_rows(ref, half, s)
    return flat[rows, :]


def _store_slot(ref, half, s, val):
    flat, rows = _slot_rows(ref, half, s)
    flat[rows, :] = val


def _load_slots(ref, half, s0):
    return jnp.concatenate([_load_slot(ref, half, s0 + i) for i in range(S1_GROUP)], axis=1)


def _fft_s1_kernel(f_ref, twr_ref, twi_ref, z_ref, o_ref, *, n1):
    for s0 in range(0, S1_ROWS, S1_GROUP):
        x = jnp.concatenate([_load_slots(z_ref, 0, s0), _load_slots(z_ref, 1, s0)], axis=0)
        y = _dft_dot(f_ref, x)
        for i in range(S1_GROUP):
            s = s0 + i
            yr = y[:n1, i * LANE:(i + 1) * LANE]
            yi = y[n1:, i * LANE:(i + 1) * LANE]
            tr, ti = twr_ref[s], twi_ref[s]
            _store_slot(o_ref, 0, s, yr * tr - yi * ti)
            _store_slot(o_ref, 1, s, yr * ti + yi * tr)


def _fft_s1(z5, col0, f, twr, twi, *, n1):
    P, _, k, n2, _ = z5.shape
    g = S1_ROWS
    ct = HY_WIDTH // LANE
    c0 = col0 // LANE
    return pl.pallas_call(
        functools.partial(_fft_s1_kernel, n1=n1),
        grid=(P, n2 // g, ct),
        in_specs=[
            _const_spec(f.shape),
            pl.BlockSpec((g, n1, LANE), lambda p, j, c: (j, 0, 0)),
            pl.BlockSpec((g, n1, LANE), lambda p, j, c: (j, 0, 0)),
            pl.BlockSpec((None, 2, k, g, LANE), lambda p, j, c: (p, 0, 0, j, c0 + c)),
        ],
        out_specs=pl.BlockSpec((None, 2, n1, g, LANE), lambda p, j, c: (p, 0, 0, j, c)),
        out_shape=jax.ShapeDtypeStruct((P, 2, n1, n2, HY_WIDTH), F32),
        compiler_params=_cparams(("parallel", "parallel", "arbitrary")),
        name="fft_s1",
    )(f, twr, twi, z5)


def _fft_mid_kernel(f_ref, g_ref, a_ref, kh_ref, o_ref, *, n2, chunk):
    for j in range(MID_SLABS):
        for c in range(0, HY_WIDTH, chunk):
            cols = slice(c, c + chunk)
            y = _dft_dot(f_ref, jnp.concatenate([a_ref[0, j, :, cols], a_ref[1, j, :, cols]], axis=0))
            br, bi = y[:n2], y[n2:]
            kr, ki = kh_ref[0, j, :, cols], kh_ref[1, j, :, cols]
            y = _dft_dot(g_ref, jnp.concatenate([br * kr - bi * ki, br * ki + bi * kr], axis=0))
            o_ref[0, j, :, cols] = y[:n2]
            o_ref[1, j, :, cols] = y[n2:]


def _fft_mid(a, kh, order, f, g):
    P, _, n1, n2, C = a.shape
    slab = lambda idx: pl.BlockSpec((None, 2, MID_SLABS, n2, C), idx)
    return pl.pallas_call(
        functools.partial(_fft_mid_kernel, n2=n2, chunk=512),
        grid=(P, n1 // MID_SLABS),
        in_specs=[_const_spec(f.shape), _const_spec(g.shape),
                  slab(lambda p, i: (p, 0, i, 0, 0)),
                  slab(lambda p, i: (order, 0, i, 0, 0))],
        out_specs=slab(lambda p, i: (p, 0, i, 0, 0)),
        out_shape=jax.ShapeDtypeStruct(a.shape, F32),
        compiler_params=_cparams(("parallel", "parallel")),
        name="fft_mid",
    )(f, g, a, kh)


def _fft_khat_kernel(f_ref, a_ref, o_ref, *, n2, chunk, inv_n):
    for c in range(0, HY_WIDTH, chunk):
        cols = slice(c, c + chunk)
        y = _dft_dot(f_ref, jnp.concatenate([a_ref[0, :, cols], a_ref[1, :, cols]], axis=0))
        o_ref[0, :, cols] = y[:n2] * inv_n
        o_ref[1, :, cols] = y[n2:] * inv_n


def _fft_khat(a, f):
    P, _, n1, n2, C = a.shape
    slab = pl.BlockSpec((None, 2, None, n2, C), lambda p, i: (p, 0, i, 0, 0))
    return pl.pallas_call(
        functools.partial(_fft_khat_kernel, n2=n2, chunk=512, inv_n=1.0 / (n1 * n2)),
        grid=(P, n1),
        in_specs=[_const_spec(f.shape), slab],
        out_specs=slab,
        out_shape=jax.ShapeDtypeStruct(a.shape, F32),
        compiler_params=_cparams(("parallel", "parallel")),
        name="fft_khat",
    )(f, a)


def _fft_s1inv_kernel(f_ref, twr_ref, twi_ref, a_ref, gate_ref, vin_ref, skip_ref, o_ref, *, k, n1):
    skip = skip_ref[...]
    for s0 in range(0, S1_ROWS, S1_GROUP):
        ar = _load_slots(a_ref, 0, s0)
        ai = _load_slots(a_ref, 1, s0)
        tr = jnp.concatenate([twr_ref[s0 + i] for i in range(S1_GROUP)], axis=1)
        ti = jnp.concatenate([twi_ref[s0 + i] for i in range(S1_GROUP)], axis=1)
        x = jnp.concatenate([ar * tr + ai * ti, ai * tr - ar * ti], axis=0)
        y = _dft_dot(f_ref, x)
        for i in range(S1_GROUP):
            s = s0 + i
            for half in range(2):
                conv = y[half * k:(half + 1) * k, i * LANE:(i + 1) * LANE]
                gated = _load_slot(gate_ref, half, s) * (conv + skip * _load_slot(vin_ref, half, s))
                _store_slot(o_ref, half, s, gated)


def _fft_s1inv(a, gate5, gate_col0, vin5, vin_col0, skip, f, twr, twi):
    P, _, n1, n2, C = a.shape
    k = n1 // 2
    g = S1_ROWS
    seq = lambda c0: pl.BlockSpec((None, 2, k, g, LANE), lambda p, j, c: (p, 0, 0, j, c0 // LANE + c))
    return pl.pallas_call(
        functools.partial(_fft_s1inv_kernel, k=k, n1=n1),
        grid=(P, n2 // g, C // LANE),
        in_specs=[
            _const_spec(f.shape),
            pl.BlockSpec((g, n1, LANE), lambda p, j, c: (j, 0, 0)),
            pl.BlockSpec((g, n1, LANE), lambda p, j, c: (j, 0, 0)),
            pl.BlockSpec((None, 2, n1, g, LANE), lambda p, j, c: (p, 0, 0, j, c)),
            seq(gate_col0), seq(vin_col0),
            pl.BlockSpec((1, LANE), lambda p, j, c: (0, c)),
        ],
        out_specs=seq(0),
        out_shape=jax.ShapeDtypeStruct((P, 2, k, n2, C), F32),
        compiler_params=_cparams(("parallel", "parallel", "arbitrary")),
        name="fft_s1inv",
    )(f, twr, twi, a, gate5, vin5, skip)


def _hyena_mix(hy, k2s, hy_skip):
    B, L, _ = hy.shape
    C = HY_WIDTH
    t = _dft_tables(L)
    n1, n2 = t["n1"], t["n2"]
    k = n1 // 2
    k2s = k2s.reshape(2, 2, k, n2, C)
    khat = _fft_khat(_fft_s1(k2s, 0, t["s1_real"], t["twr"], t["twi"], n1=n1), t["s2_filt"])
    hy5 = hy.reshape(B // 2, 2, k, n2, 3 * C)
    z5, z_col0 = hy5, 2 * C
    for order, gate_col0 in enumerate((0, C)):
        a = _fft_s1(z5, z_col0, t["s1_pair"], t["twr"], t["twi"], n1=n1)
        a = _fft_mid(a, khat, order, t["s2_fwd"], t["s2_inv"])
        z5 = _fft_s1inv(a, hy5, gate_col0, z5, z_col0, hy_skip[order:order + 1], t["s1_inv"],
                        t["twr"], t["twi"])
        z_col0 = 0
    return z5.reshape(B, L, C)


def _rope_tables(L):
    pos = jnp.arange(L, dtype=F32)
    inv = ROPE_BASE ** (-jnp.arange(0, QK_ROPE, 2, dtype=F32) / QK_ROPE)
    ang = pos[:, None] * inv[None, :]
    cos, sin = jnp.cos(ang), jnp.sin(ang)
    zeros = jnp.zeros((L, 64), F32)
    return (jnp.concatenate([cos, cos, zeros], axis=-1),
            jnp.concatenate([-sin, sin, zeros], axis=-1))


def _swap_halves(w):
    return jnp.concatenate([w[..., QK_ROPE // 2:], w[..., :QK_ROPE // 2]], axis=-1)


def _prep_weights(w_in, short_w, short_b, q_norm_g, w_uq, kv_norm_g, w_ukv, w_o_mla, w_o_hy, w_out,
                  ln1_g, ln1_b, w_ffn_up, dw_w, dw_b, w_ffn_down, ln2_g, ln2_b):
    c0, c1, c2, c3 = Q_LORA, Q_LORA + KV_LORA, Q_LORA + KV_LORA + QK_ROPE, Q_LORA + KV_LORA + QK_ROPE + IN_HY
    w_kr = w_in[:, c1:c2]
    wuq = w_uq.reshape(Q_LORA, MLA_HEADS, QK_NOPE + QK_ROPE)
    wuq_rope = wuq[..., QK_NOPE:]
    wuq = jnp.concatenate([wuq[..., :QK_NOPE], wuq_rope, _swap_halves(wuq_rope)], axis=-1)
    row = lambda a: a.reshape(1, -1)
    return dict(
        wq=w_in[:, :c0].astype(BF16), wkv=w_in[:, c0:c1].astype(BF16),
        wkr=jnp.concatenate([w_kr, _swap_halves(w_kr)], axis=-1).astype(BF16),
        why=w_in[:, c2:c3].astype(BF16), wg=w_in[:, c3:].astype(BF16),
        qg=row(q_norm_g), wuq=wuq.reshape(Q_LORA, MLA_HEADS * Q_HEAD_COLS).astype(BF16),
        kvg=row(kv_norm_g), wukv=w_ukv.astype(BF16), sw=short_w, sb=row(short_b),
        womla=w_o_mla.astype(BF16), wohy=w_o_hy.astype(BF16), wout=w_out.astype(BF16),
        ln1g=row(ln1_g), ln1b=row(ln1_b),
        wa=w_ffn_up[:, :D_FF].astype(BF16), wb=w_ffn_up[:, D_FF:].astype(BF16),
        dww=dw_w, dwb=row(dw_b), wd=w_ffn_down.astype(BF16), ln2g=row(ln2_g), ln2b=row(ln2_b),
    )


def _tiles(L):
    return dict(tm=min(512, L), tq=min(1024, L), tk=min(512, L))


def _encoder_layer(x, p, filt, hy_skip, *, tm, tq, tk):
    B, L, D = x.shape
    cs1, cs2 = _rope_tables(L)
    q, kv, kr, hy, sg = _front(x, cs1, cs2, p["wq"], p["wkv"], p["wkr"], p["why"], p["wg"], p["qg"],
                               p["wuq"], p["kvg"], p["wukv"], p["sw"], p["sb"], tm=tm)
    o = _attention(q, kv, kr, tq=tq, tk=tk)
    z = _hyena_mix(hy, _two_sided_filters(L, *filt), hy_skip)
    T = B * L
    x1 = _mid(o.reshape(T, D), z.reshape(T, D), sg.reshape(T, 2 * D), x.reshape(T, D),
              p["womla"], p["wohy"], p["wout"], p["ln1g"], p["ln1b"], tm=tm)
    return _ffn(x1.reshape(B, L, D), p["wa"], p["wb"], p["dww"], p["dwb"], p["wd"],
                p["ln2g"], p["ln2b"], tm=tm)


def kernel(x_prompt, x_sample, w_in, short_w, short_b, q_norm_g, w_uq, kv_norm_g, w_ukv, w_o_mla,
           filt_w1, filt_b1, filt_freq, filt_w2, filt_b2, filt_w3, hy_skip, w_o_hy,
           w_out, ln1_g, ln1_b, w_ffn_up, dw_w, dw_b, w_ffn_down, ln2_g, ln2_b):
    y_prompt, y_sample = x_prompt, x_sample
    for i in range(DEPTH):
        p = _prep_weights(w_in[i], short_w[i], short_b[i], q_norm_g[i], w_uq[i], kv_norm_g[i], w_ukv[i],
                          w_o_mla[i], w_o_hy[i], w_out[i], ln1_g[i], ln1_b[i], w_ffn_up[i], dw_w[i],
                          dw_b[i], w_ffn_down[i], ln2_g[i], ln2_b[i])
        filt = (filt_w1[i], filt_b1[i], filt_freq[i], filt_w2[i], filt_b2[i], filt_w3[i])
        y_prompt = _encoder_layer(y_prompt, p, filt, hy_skip[i], **_tiles(y_prompt.shape[1]))
        y_sample = _encoder_layer(y_sample, p, filt, hy_skip[i], **_tiles(y_sample.shape[1]))
    return (y_prompt, y_sample)
```
